```python
import math
import jax, jax.numpy as jnp
from jax import lax
import numpy as np

D_MODEL = 1024
BATCH = 16
SEQ = 2048
DEPTH = 2

HEAD_DIM = 64
BRANCH_W = 512
N_BRANCH = 3
MOBA_HEADS = BRANCH_W // HEAD_DIM
MOBA_BLOCK = 256
MOBA_TOPK = 3
RWKV_HEADS = BRANCH_W // HEAD_DIM
RWKV_HEAD = HEAD_DIM
LORA_DECAY = 64
LORA_AAA = 64
LORA_GATE = 160
LORA_MV = 32
RWKV_GN_EPS = 64e-5
DSA_HEADS = BRANCH_W // HEAD_DIM
DSA_KV_RANK = 256
IDX_HEADS = 8
IDX_DIM = 32
DSA_TOPK_MAX = 256
REL_BUCKETS = 32
REL_MAX_DIST = 128
D_FF = ((8 * D_MODEL + 3 * 256 - 1) // (3 * 256)) * 256
NORM_EPS = 1e-6
Q_BLOCK = 16

RWKV_SPLITS = (BRANCH_W, LORA_DECAY, BRANCH_W, BRANCH_W, LORA_AAA, LORA_GATE)
DSA_SPLITS = (BRANCH_W, DSA_KV_RANK, IDX_HEADS * IDX_DIM, IDX_DIM, IDX_HEADS)
RWKV_IN = sum(RWKV_SPLITS)
DSA_IN = sum(DSA_SPLITS)
REGION_SPLITS = (3 * BRANCH_W, RWKV_IN, DSA_IN, N_BRANCH * D_MODEL)
D_IN = sum(REGION_SPLITS)

kernel_name = 'hybrid_moba_rwkv7_dsa_gated_block'


def _split(t, widths):
    idx = np.cumsum(widths)[:-1].tolist()
    return jnp.split(t, idx, axis=-1)


def rms_norm(x, g):
    xf = x.astype(jnp.float32)
    y = xf * lax.rsqrt(jnp.mean(xf * xf, axis=-1, keepdims=True) + NORM_EPS)
    return (y * g.astype(jnp.float32)).astype(x.dtype)


def token_shift(t):
    return jnp.pad(t, ((0, 0), (1, 0), (0, 0)))[:, :-1]


def rel_bucket(dist):
    n = jnp.maximum(dist, 0)
    max_exact = REL_BUCKETS // 2
    nf = jnp.maximum(n, 1).astype(jnp.float32)
    large = max_exact + (jnp.log(nf / max_exact) / math.log(REL_MAX_DIST / max_exact)
                         * (REL_BUCKETS - max_exact)).astype(jnp.int32)
    large = jnp.minimum(large, REL_BUCKETS - 1)
    return jnp.where(n < max_exact, n, large)


def moba_attention(q, k, v, rel_tab):
    B, S, H, Dh = q.shape
    f32 = jnp.float32
    nb = -(-S // MOBA_BLOCK)
    sp = nb * MOBA_BLOCK
    pad = ((0, 0), (0, sp - S), (0, 0), (0, 0))
    qh = jnp.transpose(q, (0, 2, 1, 3))
    kb = jnp.pad(k, pad).transpose(0, 2, 1, 3).reshape(B, H, nb, MOBA_BLOCK, Dh)
    vb = jnp.pad(v, pad).transpose(0, 2, 1, 3).reshape(B, H, nb, MOBA_BLOCK, Dh)
    scale = Dh ** -0.5
    tab_t = rel_tab.T
    n_sel = min(MOBA_TOPK, nb - 1)
    own = jnp.arange(S) // MOBA_BLOCK
    h_idx = jnp.arange(H)[:, None, None]
    b_idx = jnp.arange(B)[:, None, None, None]
    hb_idx = jnp.arange(H)[None, :, None, None]
    if n_sel > 0:
        kmean = jnp.mean(kb.astype(f32), axis=3)
        gate = jnp.einsum('bhsd,bhnd->bhsn', qh.astype(f32), kmean)
        past = jnp.arange(nb)[None, :] < own[:, None]
        gate = jnp.where(past, gate, -jnp.inf)
        _, sel = lax.top_k(gate, n_sel)
        sel_valid = jnp.arange(n_sel)[None, :] < own[:, None]

    def step(c):
        t0 = c * Q_BLOCK
        tq = t0 + jnp.arange(Q_BLOCK)
        qc = lax.dynamic_slice_in_dim(qh, t0, Q_BLOCK, axis=2)
        blk = t0 // MOBA_BLOCK
        k_own = lax.dynamic_index_in_dim(kb, blk, axis=2, keepdims=False)
        v_own = lax.dynamic_index_in_dim(vb, blk, axis=2, keepdims=False)
        dist_own = tq[:, None] - (blk * MOBA_BLOCK + jnp.arange(MOBA_BLOCK))[None, :]
        lo = jnp.einsum('bhqd,bhkd->bhqk', qc, k_own, preferred_element_type=f32) * scale
        lo = jnp.where(dist_own >= 0, lo + tab_t[:, rel_bucket(dist_own)], -jnp.inf)
        if n_sel > 0:
            sc = lax.dynamic_slice_in_dim(sel, t0, Q_BLOCK, axis=2)
            valid = lax.dynamic_slice_in_dim(sel_valid, t0, Q_BLOCK, axis=0)
            npk = n_sel * MOBA_BLOCK
            kg = kb[b_idx, hb_idx, sc].reshape(B, H, Q_BLOCK, npk, Dh)
            vg = vb[b_idx, hb_idx, sc].reshape(B, H, Q_BLOCK, npk, Dh)
            kpos = (sc[..., None] * MOBA_BLOCK + jnp.arange(MOBA_BLOCK)).reshape(B, H, Q_BLOCK, npk)
            dist = tq[None, None, :, None] - kpos
            lp = jnp.einsum('bhqd,bhqkd->bhqk', qc, kg, preferred_element_type=f32) * scale
            lp = lp + tab_t[h_idx, rel_bucket(dist)]
            validk = jnp.repeat(valid, MOBA_BLOCK, axis=1)
            lp = jnp.where(validk[None, None], lp, -jnp.inf)
            p = jax.nn.softmax(jnp.concatenate([lp, lo], axis=-1), axis=-1).astype(v.dtype)
            out = (jnp.einsum('bhqk,bhqkd->bhqd', p[..., :npk], vg)
                   + jnp.einsum('bhqk,bhkd->bhqd', p[..., npk:], v_own))
        else:
            p = jax.nn.softmax(lo, axis=-1).astype(v.dtype)
            out = jnp.einsum('bhqk,bhkd->bhqd', p, v_own)
        return out

    outs = lax.map(step, jnp.arange(S // Q_BLOCK))
    return outs.transpose(1, 0, 3, 2, 4).reshape(B, S, H * Dh)


def dsa_attention(q, k, v, q_idx, k_idx, w_idx, rel_tab):
    B, S, H, Dh = q.shape
    f32 = jnp.float32
    n_keep = min(DSA_TOPK_MAX, S // 4)
    scale = Dh ** -0.5
    b_idx = jnp.arange(B)[:, None, None]
    key_pos = jnp.arange(S)

    def step(c):
        t0 = c * Q_BLOCK
        tq = t0 + jnp.arange(Q_BLOCK)
        qi = lax.dynamic_slice_in_dim(q_idx, t0, Q_BLOCK, axis=1)
        wi = lax.dynamic_slice_in_dim(w_idx, t0, Q_BLOCK, axis=1).astype(f32)
        isc = jnp.einsum('bqhd,bsd->bqhs', qi, k_idx, preferred_element_type=f32)
        isc = jnp.einsum('bqhs,bqh->bqs', jax.nn.relu(isc), wi)
        isc = jnp.where(key_pos[None, None, :] <= tq[None, :, None], isc, -jnp.inf)
        _, sel = lax.top_k(isc, n_keep)
        valid = jnp.arange(n_keep)[None, :] <= tq[:, None]
        kg = k[b_idx, sel]
        vg = v[b_idx, sel]
        qc = lax.dynamic_slice_in_dim(q, t0, Q_BLOCK, axis=1)
        lg = jnp.einsum('bqhd,bqkhd->bqhk', qc, kg, preferred_element_type=f32) * scale
        bias = rel_tab[rel_bucket(tq[None, :, None] - sel)]
        lg = jnp.where(valid[None, :, None, :], lg + jnp.moveaxis(bias, -1, 2), -jnp.inf)
        p = jax.nn.softmax(lg, axis=-1).astype(v.dtype)
        return jnp.einsum('bqhk,bqkhd->bqhd', p, vg)

    outs = lax.map(step, jnp.arange(S // Q_BLOCK))
    return outs.transpose(1, 0, 2, 3, 4).reshape(B, S, H * Dh)


def wkv7_scan(r, decay, k, v, kk, a):
    B, S, H, N = r.shape

    def step(state, inp):
        r_t, w_t, k_t, v_t, kk_t, a_t = inp
        sa = jnp.einsum('bhvk,bhk->bhv', state, -kk_t)
        state = (state * w_t[:, :, None, :] + sa[..., None] * (kk_t * a_t)[:, :, None, :]
                 + v_t[..., None] * k_t[:, :, None, :])
        return state, jnp.einsum('bhvk,bhk->bhv', state, r_t)

    xs = tuple(jnp.moveaxis(t, 1, 0) for t in (r, decay, k, v, kk, a))
    _, out = lax.scan(step, jnp.zeros((B, H, N, N), jnp.float32), xs)
    return jnp.moveaxis(out, 0, 1)


def rwkv7_time_mix(p, mu, w0, w2, a0, a2, g2, k_k, k_a, r_k, ln_w, ln_b, v_first, vres):
    B, S, _ = p.shape
    p = p.astype(jnp.float32)
    p = p + (token_shift(p) - p) * mu
    r, wd, k, v, ad, gd = _split(p, RWKV_SPLITS)
    w = -jax.nn.softplus(-(w0 + jnp.tanh(wd) @ w2)) - 0.5
    decay = jnp.exp(-jnp.exp(w))
    a = jax.nn.sigmoid(a0 + ad @ a2)
    g = jax.nn.sigmoid(gd) @ g2
    if vres is None:
        v_first = v
    else:
        v0, va, vb = vres
        v = v + (v_first - v) * jax.nn.sigmoid(v0 + (v @ va) @ vb)
    hs = lambda t: t.reshape(B, S, RWKV_HEADS, RWKV_HEAD)
    kk = hs(k * k_k)
    kk = kk / jnp.maximum(jnp.sqrt(jnp.sum(kk * kk, axis=-1, keepdims=True)), 1e-12)
    k = k * (1.0 + (a - 1.0) * k_a)
    r, k, v, a, decay = hs(r), hs(k), hs(v), hs(a), hs(decay)
    o = wkv7_scan(r, decay, k, v, kk, a)
    mean = jnp.mean(o, axis=-1, keepdims=True)
    var = jnp.mean(jnp.square(o - mean), axis=-1, keepdims=True)
    gn_w = ln_w.astype(jnp.float32).reshape(RWKV_HEADS, RWKV_HEAD)
    gn_b = ln_b.astype(jnp.float32).reshape(RWKV_HEADS, RWKV_HEAD)
    o = (o - mean) * lax.rsqrt(var + RWKV_GN_EPS) * gn_w + gn_b
    o = o + jnp.sum(r * k * r_k, axis=-1, keepdims=True) * v
    return o.reshape(B, S, BRANCH_W) * g, v_first


def setup_inputs(seed: int = 0) -> dict:
    key = jax.random.key(seed)
    ks = jax.random.split(key, 26)
    L = DEPTH
    nrm = lambda k, shape, s: s * jax.random.normal(k, shape, jnp.float32)
    return {
        'x': nrm(ks[0], (BATCH, SEQ, D_MODEL), 1.0),
        'w_in': nrm(ks[1], (L, D_MODEL, D_IN), D_MODEL ** -0.5),
        'norm_mix': 1.0 + nrm(ks[2], (L, D_MODEL), 0.05),
        'norm_ffn': 1.0 + nrm(ks[3], (L, D_MODEL), 0.05),
        'qk_norm': 1.0 + nrm(ks[4], (L, 4, HEAD_DIM), 0.05),
        'rel_bias': nrm(ks[5], (REL_BUCKETS, MOBA_HEADS + DSA_HEADS), 0.5),
        'rwkv_mu': jax.random.uniform(ks[6], (L, RWKV_IN), jnp.float32),
        'rwkv_w0': jnp.linspace(-6.0, -1.0, BRANCH_W, dtype=jnp.float32)[None, :] + nrm(ks[7], (L, BRANCH_W), 0.1),
        'rwkv_w2': nrm(ks[8], (L, LORA_DECAY, BRANCH_W), 0.5 * LORA_DECAY ** -0.5),
        'rwkv_a0': nrm(ks[9], (L, BRANCH_W), 0.1),
        'rwkv_a2': nrm(ks[10], (L, LORA_AAA, BRANCH_W), LORA_AAA ** -0.5),
        'rwkv_g2': nrm(ks[11], (L, LORA_GATE, BRANCH_W), LORA_GATE ** -0.5),
        'rwkv_kk': 0.85 + nrm(ks[12], (L, BRANCH_W), 0.05),
        'rwkv_ka': 1.0 + nrm(ks[13], (L, BRANCH_W), 0.05),
        'rwkv_rk': nrm(ks[14], (L, RWKV_HEADS, RWKV_HEAD), 0.1),
        'rwkv_ln_w': 1.0 + nrm(ks[15], (L, BRANCH_W), 0.05),
        'rwkv_ln_b': nrm(ks[16], (L, BRANCH_W), 0.02),
        'rwkv_v0': nrm(ks[17], (L - 1, BRANCH_W), 0.1),
        'rwkv_va': nrm(ks[18], (L - 1, BRANCH_W, LORA_MV), BRANCH_W ** -0.5),
        'rwkv_vb': nrm(ks[19], (L - 1, LORA_MV, BRANCH_W), LORA_MV ** -0.5),
        'dsa_kv_norm': 1.0 + nrm(ks[20], (L, DSA_KV_RANK), 0.05),
        'dsa_kv_up': nrm(ks[21], (L, DSA_KV_RANK, 2 * BRANCH_W), DSA_KV_RANK ** -0.5),
        'w_branch': nrm(ks[22], (L, N_BRANCH, BRANCH_W, D_MODEL), BRANCH_W ** -0.5),
        'w_o': nrm(ks[23], (L, D_MODEL, D_MODEL), D_MODEL ** -0.5),
        'w_ffn_in': nrm(ks[24], (L, D_MODEL, 2 * D_FF), D_MODEL ** -0.5),
        'w_ffn_out': nrm(ks[25], (L, D_FF, D_MODEL), D_FF ** -0.5),
    }


def reference(x, w_in, norm_mix, norm_ffn, qk_norm, rel_bias, rwkv_mu, rwkv_w0, rwkv_w2, rwkv_a0,
              rwkv_a2, rwkv_g2, rwkv_kk, rwkv_ka, rwkv_rk, rwkv_ln_w, rwkv_ln_b, rwkv_v0, rwkv_va,
              rwkv_vb, dsa_kv_norm, dsa_kv_up, w_branch, w_o, w_ffn_in, w_ffn_out):
    B, S, D = x.shape
    moba_tab = rel_bias[:, :MOBA_HEADS]
    dsa_tab = rel_bias[:, MOBA_HEADS:]
    heads = lambda t, n: t.reshape(B, S, n, HEAD_DIM)
    v_first = None
    for l in range(DEPTH):
        h = rms_norm(x, norm_mix[l])
        p = h @ w_in[l]
        p_moba, p_rwkv, p_dsa, p_gate = _split(p, REGION_SPLITS)
        mq, mk, mv = _split(p_moba, (BRANCH_W, BRANCH_W, BRANCH_W))
        mq = rms_norm(heads(mq, MOBA_HEADS), qk_norm[l, 0])
        mk = rms_norm(heads(mk, MOBA_HEADS), qk_norm[l, 1])
        o_moba = moba_attention(mq, mk, heads(mv, MOBA_HEADS), moba_tab)
        vres = None if l == 0 else (rwkv_v0[l - 1], rwkv_va[l - 1], rwkv_vb[l - 1])
        o_rwkv, v_first = rwkv7_time_mix(p_rwkv, rwkv_mu[l], rwkv_w0[l], rwkv_w2[l], rwkv_a0[l],
                                         rwkv_a2[l], rwkv_g2[l], rwkv_kk[l], rwkv_ka[l], rwkv_rk[l],
                                         rwkv_ln_w[l], rwkv_ln_b[l], v_first, vres)
        dq, dc, diq, dik, diw = _split(p_dsa, DSA_SPLITS)
        dk, dv = _split(rms_norm(dc, dsa_kv_norm[l]) @ dsa_kv_up[l], (BRANCH_W, BRANCH_W))
        dq = rms_norm(heads(dq, DSA_HEADS), qk_norm[l, 2])
        dk = rms_norm(heads(dk, DSA_HEADS), qk_norm[l, 3])
        o_dsa = dsa_attention(dq, dk, heads(dv, DSA_HEADS), diq.reshape(B, S, IDX_HEADS, IDX_DIM),
                              dik, diw, dsa_tab)
        o = jnp.stack([o_moba, o_rwkv.astype(x.dtype), o_dsa], axis=2)
        y = jnp.einsum('bsnc,ncd->bsnd', o, w_branch[l])
        gate = jax.nn.sigmoid(p_gate.reshape(B, S, N_BRANCH, D))
        x = x + jnp.sum(gate * y, axis=2) @ w_o[l]
        fg, fu = _split(rms_norm(x, norm_ffn[l]) @ w_ffn_in[l], (D_FF, D_FF))
        x = x + (jax.nn.silu(fg) * fu) @ w_ffn_out[l]
    return x
```

```python
import functools
import math

import numpy as np
import jax
import jax.numpy as jnp
from jax import lax
from jax.experimental import pallas as pl
from jax.experimental.pallas import tpu as pltpu

F32 = jnp.float32
BF16 = jnp.bfloat16
HIGHEST = lax.Precision.HIGHEST

D_MODEL = 1024
HEAD_DIM = 64
BRANCH_W = 512
N_HEADS = BRANCH_W // HEAD_DIM
MOBA_BLOCK = 256
MOBA_TOPK = 3
LORA_DECAY = 64
LORA_AAA = 64
LORA_GATE = 160
LORA_MV = 32
RWKV_GN_EPS = 64e-5
DSA_KV_RANK = 256
IDX_HEADS = 8
IDX_DIM = 32
DSA_TOPK_MAX = 256
REL_BUCKETS = 32
REL_MAX_DIST = 128
D_FF = 2816
NORM_EPS = 1e-6

ATT_BLOCK = 256
RWKV_CHUNK = 64
INT_MIN = -(2 ** 31)
NEG_INF = float("-inf")
VMEM_LIMIT = 52 * 1024 * 1024

OFF_GATE = 0
OFF_MQ, OFF_MK, OFF_MV = 3072, 3584, 4096
OFF_RW = 4608
OFF_DQ, OFF_DC, OFF_DIQ, OFF_DIKW = 6656, 7168, 7424, 7680
N_PROJ = 7808
RW_W = 2048


def _nt(a, b, precision=None):
    return lax.dot_general(a, b, (((1,), (1,)), ((), ())), precision=precision,
                           preferred_element_type=F32)


def _tn(a, b, precision=None):
    return lax.dot_general(a, b, (((0,), (0,)), ((), ())), precision=precision,
                           preferred_element_type=F32)


def _mm(a, b, precision=None):
    return jnp.dot(a, b, precision=precision, preferred_element_type=F32)


def _group_sum(t, gmat):
    hi = t.astype(BF16)
    lo = (t - hi.astype(F32)).astype(BF16)
    return _mm(hi, gmat) + _mm(lo, gmat)


def _head_rms(t, gain, gmat):
    ms = _group_sum(t * t, gmat) * (1.0 / HEAD_DIM)
    return t * lax.rsqrt(ms + NORM_EPS) * gain


def _resident(shape):
    nd = len(shape)
    return pl.BlockSpec(shape, lambda *_: (0,) * nd, pipeline_mode=pl.Buffered(1))


def _proj_kernel(x_ref, g_ref, w_ref, kvn_ref, kvup_ref, qkn_ref, gmat_ref,
                 gate_o, mq_o, mk_o, mv_o, rw_o, dq_o, dk_o, dv_o, diq_o, dikw_o):
    x = x_ref[...]
    ms = jnp.mean(x * x, axis=-1, keepdims=True)
    h = (x * lax.rsqrt(ms + NORM_EPS) * g_ref[...]).astype(BF16)
    gmat = gmat_ref[...]

    def proj(off, width):
        return _mm(h, w_ref[:, off:off + width])

    gate_o[...] = proj(OFF_GATE, 3 * D_MODEL).astype(BF16)
    mq_o[...] = _head_rms(proj(OFF_MQ, BRANCH_W), qkn_ref[0:1, :], gmat).astype(BF16)
    mk_o[...] = _head_rms(proj(OFF_MK, BRANCH_W), qkn_ref[1:2, :], gmat).astype(BF16)
    mv_o[...] = proj(OFF_MV, BRANCH_W).astype(BF16)
    rw_o[...] = proj(OFF_RW, RW_W)
    dq_o[...] = _head_rms(proj(OFF_DQ, BRANCH_W), qkn_ref[2:3, :], gmat).astype(BF16)
    dc = proj(OFF_DC, DSA_KV_RANK)
    dcn = dc * lax.rsqrt(jnp.mean(dc * dc, axis=-1, keepdims=True) + NORM_EPS) * kvn_ref[...]
    kv = _mm(dcn.astype(BF16), kvup_ref[...])
    dk_o[...] = _head_rms(kv[:, :BRANCH_W], qkn_ref[3:4, :], gmat).astype(BF16)
    dv_o[...] = kv[:, BRANCH_W:].astype(BF16)
    diq_o[...] = proj(OFF_DIQ, IDX_HEADS * IDX_DIM)
    dikw_o[...] = proj(OFF_DIKW, 128)


def _proj_call(x2, g, w_perm, kvn, kvup, qkn, gmat, tm=256):
    T = x2.shape[0]
    row = lambda w: pl.BlockSpec((tm, w), lambda i: (i, 0))
    outs = [
        ((T, 3 * D_MODEL), BF16), ((T, BRANCH_W), BF16), ((T, BRANCH_W), BF16), ((T, BRANCH_W), BF16),
        ((T, RW_W), F32), ((T, BRANCH_W), BF16), ((T, BRANCH_W), BF16), ((T, BRANCH_W), BF16),
        ((T, IDX_HEADS * IDX_DIM), F32), ((T, 128), F32),
    ]
    return pl.pallas_call(
        _proj_kernel,
        grid=(T // tm,),
        in_specs=[row(D_MODEL), _resident(g.shape), _resident(w_perm.shape), _resident(kvn.shape),
                  _resident(kvup.shape), _resident(qkn.shape), _resident(gmat.shape)],
        out_specs=[row(s[1]) for s, _ in outs],
        out_shape=[jax.ShapeDtypeStruct(s, d) for s, d in outs],
        compiler_params=pltpu.CompilerParams(dimension_semantics=("arbitrary",),
                                             vmem_limit_bytes=VMEM_LIMIT),
        name="proj",
    )(x2, g, w_perm, kvn, kvup, qkn, gmat)


def _flash_step(q, kj, vj, logit_add, m, l, acc):
    s = _nt(q, kj) * (HEAD_DIM ** -0.5) + logit_add
    m_new = jnp.maximum(m, jnp.max(s, axis=-1, keepdims=True))
    alpha = jnp.exp(m - m_new)
    p = jnp.exp(s - m_new)
    l = alpha * l + jnp.sum(p, axis=-1, keepdims=True)
    acc = alpha * acc + _mm(p.astype(BF16), vj)
    return m_new, l, acc


def _tile_bias(bias_ref, h, j, i):
    own = bias_ref[h, :, ATT_BLOCK:]
    prev = bias_ref[h, :, :ATT_BLOCK]
    far = bias_ref[h, ATT_BLOCK - 1:ATT_BLOCK, 0:1]
    return jnp.where(j == i, own, jnp.where(j == i - 1, prev, far))


def _moba_kernel(q_ref, k_ref, v_ref, bias_ref, o_ref, kmean_ref, *, nb):
    i = pl.program_id(2)

    @pl.when(i == 0)
    def _():
        for j in range(nb):
            blk = k_ref[j * ATT_BLOCK:(j + 1) * ATT_BLOCK, :].astype(F32)
            kmean_ref[j:j + 1, :] = jnp.mean(blk, axis=0, keepdims=True)

    col = lax.broadcasted_iota(jnp.int32, (ATT_BLOCK, nb), 1)
    n_sel = min(MOBA_TOPK, nb - 1)
    outs = []
    for hh in range(2):
        lanes = slice(hh * HEAD_DIM, (hh + 1) * HEAD_DIM)
        q = q_ref[:, lanes]
        gate = _nt(q.astype(F32), kmean_ref[:, lanes], precision=HIGHEST)
        past = col < i
        rank = jnp.zeros((ATT_BLOCK, nb), F32)
        for jp in range(nb):
            gj = gate[:, jp:jp + 1]
            ahead = (gj > gate) | ((gj == gate) & (jp < col))
            rank = rank + jnp.where(ahead & (jp < i), 1.0, 0.0)
        sel = jnp.where(past & (rank < n_sel), 1.0, 0.0)

        def body(j, carry, q=q, sel=sel, lanes=lanes, hh=hh):
            m, l, acc = carry
            rows = pl.ds(pl.multiple_of(j * ATT_BLOCK, ATT_BLOCK), ATT_BLOCK)
            kj = k_ref[rows, lanes]
            vj = v_ref[rows, lanes]
            chosen = jnp.sum(jnp.where(col == j, sel, 0.0), axis=-1, keepdims=True) > 0.0
            add = jnp.where(chosen | (j == i), _tile_bias(bias_ref, hh, j, i), NEG_INF)
            return _flash_step(q, kj, vj, add, m, l, acc)

        init = (jnp.full((ATT_BLOCK, 1), -1e30, F32), jnp.zeros((ATT_BLOCK, 1), F32),
                jnp.zeros((ATT_BLOCK, HEAD_DIM), F32))
        m, l, acc = lax.fori_loop(0, i + 1, body, init)
        outs.append(acc / l)
    o_ref[...] = jnp.concatenate(outs, axis=-1).astype(o_ref.dtype)


def _moba_call(mq, mk, mv, bias, B, S):
    nb = S // ATT_BLOCK
    T = B * S
    qspec = pl.BlockSpec((ATT_BLOCK, 128), lambda b, g, i: (b * nb + i, g))
    kvspec = pl.BlockSpec((S, 128), lambda b, g, i: (b, g))
    return pl.pallas_call(
        functools.partial(_moba_kernel, nb=nb),
        grid=(B, N_HEADS // 2, nb),
        in_specs=[qspec, kvspec, kvspec,
                  pl.BlockSpec((2, ATT_BLOCK, 2 * ATT_BLOCK), lambda b, g, i: (g, 0, 0))],
        out_specs=qspec,
        out_shape=jax.ShapeDtypeStruct((T, BRANCH_W), BF16),
        scratch_shapes=[pltpu.VMEM((nb, 128), F32)],
        compiler_params=pltpu.CompilerParams(
            dimension_semantics=("arbitrary", "arbitrary", "arbitrary"),
            vmem_limit_bytes=VMEM_LIMIT),
        name="moba",
    )(mq, mk, mv, bias)


def _dsa_kernel(q_ref, k_ref, v_ref, qi_ref, wq_ref, kx_ref, bias_ref, o_ref,
                keys_ref, mask_ref, *, n_keep, n_idx_bits):
    i = pl.program_id(1)
    Q = ATT_BLOCK
    row = lax.broadcasted_iota(jnp.int32, (Q, Q), 0)
    colt = lax.broadcasted_iota(jnp.int32, (Q, Q), 1)

    def blk(j):
        return pl.ds(pl.multiple_of(j * Q, Q), Q)

    def score_body(j, _):
        kx = kx_ref[blk(j), 0:IDX_DIM]
        acc = jnp.zeros((Q, Q), F32)
        for h in range(IDX_HEADS):
            s_h = _nt(qi_ref[:, h * IDX_DIM:(h + 1) * IDX_DIM], kx, precision=HIGHEST)
            acc = acc + jnp.maximum(s_h, 0.0) * wq_ref[:, IDX_DIM + h:IDX_DIM + h + 1]
        acc = jnp.where(acc == 0.0, 0.0, acc)
        bits = pltpu.bitcast(acc, jnp.int32)
        key = jnp.where(bits < 0, bits ^ jnp.int32(0x7FFFFFFF), bits)
        causal = (j * Q + colt) <= (i * Q + row)
        keys_ref[:, blk(j)] = jnp.where(causal, key, jnp.int32(INT_MIN))
        return 0

    lax.fori_loop(0, i + 1, score_body, 0)

    def count(pred):
        def body(j, acc):
            hit = jnp.where(pred(keys_ref[:, blk(j)], j * Q), 1.0, 0.0)
            return acc + hit[:, :128] + hit[:, 128:]
        acc = lax.fori_loop(0, i + 1, body, jnp.zeros((Q, 128), F32))
        return jnp.sum(acc, axis=-1, keepdims=True)

    def thr_body(it, thr):
        cand = thr + jnp.left_shift(jnp.int32(1), 31 - it)
        cnt = count(lambda kb, c0: kb >= cand)
        return jnp.where(cnt >= n_keep, cand, thr)

    thr = lax.fori_loop(0, 32, thr_body, jnp.full((Q, 1), INT_MIN, jnp.int32))

    need = n_keep - count(lambda kb, c0: kb > thr)

    def cut_body(it, cut):
        cand = cut + jnp.left_shift(jnp.int32(1), n_idx_bits - 1 - it)
        cnt = count(lambda kb, c0: (kb == thr) & ((c0 + colt) < cand))
        return jnp.where(cnt <= need, cand, cut)

    cut = lax.fori_loop(0, n_idx_bits, cut_body, jnp.zeros((Q, 1), jnp.int32))

    def mask_body(j, _):
        kb = keys_ref[:, blk(j)]
        keep = (kb > thr) | ((kb == thr) & ((j * Q + colt) < cut) & (kb > jnp.int32(INT_MIN)))
        mask_ref[:, blk(j)] = jnp.where(keep, 0.0, NEG_INF)
        return 0

    lax.fori_loop(0, i + 1, mask_body, 0)

    for h in range(N_HEADS):
        lanes = slice(h * HEAD_DIM, (h + 1) * HEAD_DIM)
        q = q_ref[:, lanes]

        def body(j, carry, q=q, lanes=lanes, h=h):
            m, l, acc = carry
            add = mask_ref[:, blk(j)] + _tile_bias(bias_ref, h, j, i)
            return _flash_step(q, k_ref[blk(j), lanes], v_ref[blk(j), lanes], add, m, l, acc)

        init = (jnp.full((Q, 1), -1e30, F32), jnp.zeros((Q, 1), F32), jnp.zeros((Q, HEAD_DIM), F32))
        m, l, acc = lax.fori_loop(0, i + 1, body, init)
        o_ref[:, lanes] = (acc / l).astype(o_ref.dtype)


def _dsa_call(dq, dk, dv, diq, dikw, bias, B, S):
    nq = S // ATT_BLOCK
    T = B * S
    n_keep = min(DSA_TOPK_MAX, S // 4)
    qrow = lambda w: pl.BlockSpec((ATT_BLOCK, w), lambda b, i: (b * nq + i, 0))
    seq = lambda w: pl.BlockSpec((S, w), lambda b, i: (b, 0))
    return pl.pallas_call(
        functools.partial(_dsa_kernel, n_keep=n_keep, n_idx_bits=int(math.log2(S)) + 1),
        grid=(B, nq),
        in_specs=[qrow(BRANCH_W), seq(BRANCH_W), seq(BRANCH_W), qrow(IDX_HEADS * IDX_DIM),
                  qrow(128), seq(128), _resident(bias.shape)],
        out_specs=qrow(BRANCH_W),
        out_shape=jax.ShapeDtypeStruct((T, BRANCH_W), BF16),
        scratch_shapes=[pltpu.VMEM((ATT_BLOCK, S), jnp.int32), pltpu.VMEM((ATT_BLOCK, S), F32)],
        compiler_params=pltpu.CompilerParams(dimension_semantics=("arbitrary", "arbitrary"),
                                             vmem_limit_bytes=VMEM_LIMIT),
        name="dsa",
    )(dq, dk, dv, diq, dikw, dikw, bias)


def _rwkv_pre_kernel(*refs, tiles_per_seq, has_vres):
    if has_vres:
        (rw_ref, prev_ref, mu_ref, vec_ref, w2_ref, a2_ref, g2_ref, gmat_ref,
         vf_ref, v0_ref, va_ref, vb_ref, r_o, lw_o, k_o, v_o, a_o, b_o, g_o) = refs
    else:
        (rw_ref, prev_ref, mu_ref, vec_ref, w2_ref, a2_ref, g2_ref, gmat_ref,
         r_o, lw_o, k_o, v_o, a_o, b_o, g_o) = refs
    i = pl.program_id(0)
    p = rw_ref[...]
    tm = p.shape[0]
    last = prev_ref[7:8, :]
    last = jnp.where(i % tiles_per_seq == 0, 0.0, last)
    rid = lax.broadcasted_iota(jnp.int32, p.shape, 0)
    shifted = jnp.where(rid == 0, last, pltpu.roll(p, 1, 0))
    p = p + (shifted - p) * mu_ref[...]
    r = p[:, 0:512]
    k = p[:, 512:1024]
    v = p[:, 1024:1536]
    wd = p[:, 1536:1536 + LORA_DECAY]
    ad = p[:, 1664:1664 + LORA_AAA]
    gd = p[:, 1792:2048]
    w0, a0, k_k, k_a = vec_ref[0:1, :], vec_ref[1:2, :], vec_ref[2:3, :], vec_ref[3:4, :]
    z = -(w0 + _mm(jnp.tanh(wd), w2_ref[...], HIGHEST))
    softplus = jnp.maximum(z, 0.0) + jnp.log(1.0 + jnp.exp(-jnp.abs(z)))
    w = -softplus - 0.5
    lw_o[...] = -jnp.exp(w)
    a = jax.nn.sigmoid(a0 + _mm(ad, a2_ref[...], HIGHEST))
    g_o[...] = _mm(jax.nn.sigmoid(gd), g2_ref[...], HIGHEST)
    if has_vres:
        mix = jax.nn.sigmoid(v0_ref[...] + _mm(_mm(v, va_ref[...], HIGHEST), vb_ref[...], HIGHEST))
        v = v + (vf_ref[...] - v) * mix
    kk = k * k_k
    nrm = jnp.sqrt(_group_sum(kk * kk, gmat_ref[...]))
    kk = kk / jnp.maximum(nrm, 1e-12)
    r_o[...] = r
    k_o[...] = k * (1.0 + (a - 1.0) * k_a)
    v_o[...] = v
    a_o[...] = -kk
    b_o[...] = kk * a


def _rwkv_pre_call(rw, mu, vec, w2, a2, g2, gmat, S, vres=None, tm=256):
    T = rw.shape[0]
    tiles_per_seq = S // tm
    row = lambda w: pl.BlockSpec((tm, w), lambda i: (i, 0))
    prev = pl.BlockSpec((8, RW_W), lambda i: (jnp.maximum(i * (tm // 8) - 1, 0), 0))
    ins = [rw, rw, mu, vec, w2, a2, g2, gmat]
    specs = [row(RW_W), prev] + [_resident(t.shape) for t in ins[2:]]
    if vres is not None:
        vf, v0, va, vb = vres
        ins += [vf, v0, va, vb]
        specs += [row(BRANCH_W), _resident(v0.shape), _resident(va.shape), _resident(vb.shape)]
    return pl.pallas_call(
        functools.partial(_rwkv_pre_kernel, tiles_per_seq=tiles_per_seq, has_vres=vres is not None),
        grid=(T // tm,),
        in_specs=specs,
        out_specs=[row(BRANCH_W)] * 7,
        out_shape=[jax.ShapeDtypeStruct((T, BRANCH_W), F32)] * 7,
        compiler_params=pltpu.CompilerParams(dimension_semantics=("arbitrary",),
                                             vmem_limit_bytes=VMEM_LIMIT),
        name="rwkv_pre",
    )(*ins)


def _rwkv_scan_kernel(r_ref, lw_ref, k_ref, v_ref, a_ref, b_ref, g_ref, vec_ref, gmat_ref, tri_ref,
                      o_ref, state_ref, oraw_ref, *, n_chunks):
    C = RWKV_CHUNK
    N = HEAD_DIM

    @pl.when(pl.program_id(1) == 0)
    def _():
        state_ref[...] = jnp.zeros_like(state_ref)

    rr = lax.broadcasted_iota(jnp.int32, (C, C), 0)
    cc = lax.broadcasted_iota(jnp.int32, (C, C), 1)
    strict = rr > cc
    incl = rr >= cc
    eye = jnp.where(rr == cc, 1.0, 0.0)

    def chunk_body(c, _):
        rows = pl.ds(pl.multiple_of(c * C, C), C)
        lw = lw_ref[rows, :]
        cum = _mm(tri_ref[...], lw, HIGHEST)
        cum_last = cum[C - 1:C, :]
        e_pos = jnp.exp(cum)
        e_neg = jnp.exp(-cum)
        e_end = jnp.exp(cum_last - cum)
        r = r_ref[rows, :]
        k = k_ref[rows, :]
        v = v_ref[rows, :]
        a = a_ref[rows, :]
        b = b_ref[rows, :]
        at = a * jnp.exp(cum - lw)
        rt = r * e_pos
        bt = b * e_neg
        kt = k * e_neg
        bp = b * e_end
        kp = k * e_end
        p_end = jnp.exp(cum_last)
        for h in range(N_HEADS):
            ln = slice(h * N, (h + 1) * N)
            lhs = jnp.concatenate([at[:, ln], rt[:, ln]], axis=0).astype(BF16)
            rhs = jnp.concatenate([bt[:, ln], kt[:, ln]], axis=0).astype(BF16)
            aall = _nt(lhs, rhs)
            a_ab = jnp.where(strict, aall[:C, :C], 0.0)
            a_ak = jnp.where(strict, aall[:C, C:], 0.0)
            a_rb = jnp.where(incl, aall[C:, :C], 0.0)
            a_rk = jnp.where(incl, aall[C:, C:], 0.0)
            x = a_ab
            tinv = eye + x
            for _ in range(int(math.log2(C)) - 1):
                xb = x.astype(BF16)
                x = _mm(xb, xb)
                tinv = tinv + _mm(tinv.astype(BF16), x.astype(BF16))
            tb = tinv.astype(BF16)
            vh = v[:, ln]
            vb16 = vh.astype(BF16)
            w_t = _mm(tb, at[:, ln].astype(BF16))
            u_loc = _mm(tb, _mm(a_ak.astype(BF16), vb16).astype(BF16))
            o_loc = _mm(a_rk.astype(BF16), vb16)
            s_h = state_ref[h]
            xs = _nt(jnp.concatenate([w_t, rt[:, ln]], axis=0).astype(BF16), s_h.astype(BF16))
            u = xs[:C] + u_loc
            o = xs[C:] + _mm(a_rb.astype(BF16), u.astype(BF16)) + o_loc
            upd = _tn(jnp.concatenate([u, vh], axis=0).astype(BF16),
                      jnp.concatenate([bp[:, ln], kp[:, ln]], axis=0).astype(BF16))
            state_ref[h] = s_h * p_end[:, ln] + upd
            oraw_ref[rows, ln] = o
        return 0

    lax.fori_loop(0, n_chunks, chunk_body, 0)

    gmat = gmat_ref[...]
    o = oraw_ref[...]
    ln_w, ln_b, r_k = vec_ref[0:1, :], vec_ref[1:2, :], vec_ref[2:3, :]
    mean = _group_sum(o, gmat) * (1.0 / N)
    d = o - mean
    var = _group_sum(d * d, gmat) * (1.0 / N)
    o = d * lax.rsqrt(var + RWKV_GN_EPS) * ln_w + ln_b
    bonus = _group_sum(r_ref[...] * k_ref[...] * r_k, gmat)
    o_ref[...] = ((o + bonus * v_ref[...]) * g_ref[...]).astype(o_ref.dtype)


def _rwkv_scan_call(r, lw, k, v, a, b, g, vec, gmat, tri, B, S, tt=256):
    T = B * S
    nt = S // tt
    row = pl.BlockSpec((tt, BRANCH_W), lambda bi, t: (bi * nt + t, 0))
    return pl.pallas_call(
        functools.partial(_rwkv_scan_kernel, n_chunks=tt // RWKV_CHUNK),
        grid=(B, nt),
        in_specs=[row] * 7 + [_resident(vec.shape), _resident(gmat.shape), _resident(tri.shape)],
        out_specs=row,
        out_shape=jax.ShapeDtypeStruct((T, BRANCH_W), BF16),
        scratch_shapes=[pltpu.VMEM((N_HEADS, HEAD_DIM, HEAD_DIM), F32),
                        pltpu.VMEM((tt, BRANCH_W), F32)],
        compiler_params=pltpu.CompilerParams(dimension_semantics=("arbitrary", "arbitrary"),
                                             vmem_limit_bytes=VMEM_LIMIT),
        name="rwkv_scan",
    )(r, lw, k, v, a, b, g, vec, gmat, tri)


def _merge_ffn_kernel(x_ref, gate_ref, oa_ref, ob_ref, oc_ref, wbr_ref, wo_ref, nf_ref,
                      wfi_ref, wfo_ref, o_ref, act_ref, *, ff_chunk):
    mix = None
    for n, br in enumerate((oa_ref, ob_ref, oc_ref)):
        y = _mm(br[...], wbr_ref[n])
        gt = jax.nn.sigmoid(gate_ref[:, n * D_MODEL:(n + 1) * D_MODEL].astype(F32))
        mix = gt * y if mix is None else mix + gt * y
    x = x_ref[...] + _mm(mix.astype(BF16), wo_ref[...])
    ms = jnp.mean(x * x, axis=-1, keepdims=True)
    h = (x * lax.rsqrt(ms + NORM_EPS) * nf_ref[...]).astype(BF16)
    for c in range(0, D_FF, ff_chunk):
        fg = _mm(h, wfi_ref[:, c:c + ff_chunk])
        fu = _mm(h, wfi_ref[:, D_FF + c:D_FF + c + ff_chunk])
        act_ref[:, c:c + ff_chunk] = (fg * jax.nn.sigmoid(fg) * fu).astype(BF16)
    o_ref[...] = x + _mm(act_ref[...], wfo_ref[...])


def _merge_ffn_call(x2, gate, oa, ob, oc, wbr, wo, nf, wfi, wfo, tm=512, ff_chunk=256):
    T = x2.shape[0]
    row = lambda w: pl.BlockSpec((tm, w), lambda i: (i, 0))
    return pl.pallas_call(
        functools.partial(_merge_ffn_kernel, ff_chunk=ff_chunk),
        grid=(T // tm,),
        in_specs=[row(D_MODEL), row(3 * D_MODEL), row(BRANCH_W), row(BRANCH_W), row(BRANCH_W),
                  _resident(wbr.shape), _resident(wo.shape), _resident(nf.shape),
                  _resident(wfi.shape), _resident(wfo.shape)],
        out_specs=row(D_MODEL),
        out_shape=jax.ShapeDtypeStruct((T, D_MODEL), F32),
        scratch_shapes=[pltpu.VMEM((tm, D_FF), BF16)],
        compiler_params=pltpu.CompilerParams(dimension_semantics=("arbitrary",),
                                             vmem_limit_bytes=VMEM_LIMIT),
        name="merge_ffn",
    )(x2, gate, oa, ob, oc, wbr, wo, nf, wfi, wfo)


def _bucket_table():
    n = np.arange(2 * ATT_BLOCK)
    max_exact = REL_BUCKETS // 2
    nf = np.maximum(n, 1).astype(np.float32)
    large = max_exact + (np.log(nf / max_exact) / math.log(REL_MAX_DIST / max_exact)
                         * (REL_BUCKETS - max_exact)).astype(np.int32)
    large = np.minimum(large, REL_BUCKETS - 1)
    return np.where(n < max_exact, n, large).astype(np.int32)


def _bias_strips(rel_bias):
    r = np.arange(ATT_BLOCK)[:, None]
    c = np.arange(2 * ATT_BLOCK)[None, :]
    dist = np.where(c < ATT_BLOCK, ATT_BLOCK + r - c, r - (c - ATT_BLOCK))
    bucket = _bucket_table()[np.maximum(dist, 0)]
    strips = jnp.transpose(rel_bias[bucket], (2, 0, 1))
    return jnp.where(jnp.asarray(dist >= 0)[None], strips, NEG_INF)


def _pad_cols(t, width):
    return jnp.pad(t, ((0, 0), (0, width - t.shape[1])))


def _pad_rows(t, height):
    return jnp.pad(t, ((0, height - t.shape[0]), (0, 0)))


def _permute_w_in(w):
    o = 0
    def take(n):
        nonlocal o
        s = w[:, o:o + n]
        o += n
        return s
    mq, mk, mv = take(512), take(512), take(512)
    r, wd, k, v, ad, gd = take(512), take(LORA_DECAY), take(512), take(512), take(LORA_AAA), take(LORA_GATE)
    dq, dc, diq, dik, diw = take(512), take(DSA_KV_RANK), take(256), take(IDX_DIM), take(IDX_HEADS)
    gate = take(3 * D_MODEL)
    pieces = [gate, mq, mk, mv, r, k, v, _pad_cols(wd, 128), _pad_cols(ad, 128), _pad_cols(gd, 256),
              dq, dc, diq, _pad_cols(jnp.concatenate([dik, diw], axis=1), 128)]
    return jnp.concatenate(pieces, axis=1).astype(BF16)


def _permute_mu(mu):
    r, wd, k, v, ad, gd = jnp.split(mu, np.cumsum([512, LORA_DECAY, 512, 512, LORA_AAA])[:].tolist())
    z = lambda n: jnp.zeros((n,), F32)
    return jnp.concatenate([r, k, v, wd, z(64), ad, z(64), gd, z(256 - LORA_GATE)])[None, :]


def kernel(x, w_in, norm_mix, norm_ffn, qk_norm, rel_bias, rwkv_mu, rwkv_w0, rwkv_w2, rwkv_a0, rwkv_a2, rwkv_g2, rwkv_kk, rwkv_ka, rwkv_rk, rwkv_ln_w, rwkv_ln_b, rwkv_v0, rwkv_va, rwkv_vb, dsa_kv_norm, dsa_kv_up, w_branch, w_o, w_ffn_in, w_ffn_out):
    B, S, D = x.shape
    depth = w_in.shape[0]
    T = B * S
    strips = _bias_strips(rel_bias)
    moba_bias, dsa_bias = strips[:N_HEADS], strips[N_HEADS:]
    head_of_lane = np.arange(BRANCH_W) // HEAD_DIM
    gmat = jnp.asarray(head_of_lane[:, None] == head_of_lane[None, :], BF16)
    tri = jnp.asarray(np.tril(np.ones((RWKV_CHUNK, RWKV_CHUNK), np.float32)))
    x2 = x.reshape(T, D)
    v_first = None
    for l in range(depth):
        qkn = jnp.tile(qk_norm[l], (1, N_HEADS))
        (gate, mq, mk, mv, rw, dq, dk, dv, diq, dikw) = _proj_call(
            x2, norm_mix[l][None, :], _permute_w_in(w_in[l]), dsa_kv_norm[l][None, :],
            dsa_kv_up[l].astype(BF16), qkn, gmat)
        o_moba = _moba_call(mq, mk, mv, moba_bias, B, S)
        o_dsa = _dsa_call(dq, dk, dv, diq, dikw, dsa_bias, B, S)
        vec = jnp.stack([rwkv_w0[l], rwkv_a0[l], rwkv_kk[l], rwkv_ka[l]])
        vres = None
        if l > 0:
            vres = (v_first, rwkv_v0[l - 1][None, :], _pad_cols(rwkv_va[l - 1], 128),
                    _pad_rows(rwkv_vb[l - 1], 128))
        r, lw, k, v, a, b, g = _rwkv_pre_call(
            rw, _permute_mu(rwkv_mu[l]), vec, rwkv_w2[l], rwkv_a2[l], _pad_rows(rwkv_g2[l], 256),
            gmat, S, vres)
        if l == 0:
            v_first = v
        vec2 = jnp.stack([rwkv_ln_w[l], rwkv_ln_b[l], rwkv_rk[l].reshape(-1)])
        o_rwkv = _rwkv_scan_call(r, lw, k, v, a, b, g, vec2, gmat, tri, B, S)
        x2 = _merge_ffn_call(x2, gate, o_moba, o_rwkv, o_dsa, w_branch[l].astype(BF16),
                             w_o[l].astype(BF16), norm_ffn[l][None, :],
                             w_ffn_in[l].astype(BF16), w_ffn_out[l].astype(BF16))
    return x2.reshape(B, S, D)
```

```python
import functools
import math

import numpy as np
import jax
import jax.numpy as jnp
from jax import lax
from jax.experimental import pallas as pl
from jax.experimental.pallas import tpu as pltpu

F32 = jnp.float32
BF16 = jnp.bfloat16
HIGHEST = lax.Precision.HIGHEST

D_MODEL = 1024
HEAD_DIM = 64
BRANCH_W = 512
N_HEADS = BRANCH_W // HEAD_DIM
MOBA_BLOCK = 256
MOBA_TOPK = 3
LORA_DECAY = 64
LORA_AAA = 64
LORA_GATE = 160
LORA_MV = 32
RWKV_GN_EPS = 64e-5
DSA_KV_RANK = 256
IDX_HEADS = 8
IDX_DIM = 32
DSA_TOPK_MAX = 256
REL_BUCKETS = 32
REL_MAX_DIST = 128
D_FF = 2816
NORM_EPS = 1e-6

ATT_BLOCK = 256
RWKV_CHUNK = 64
RWKV_TILE = 256
INT_MIN = -(2 ** 31)
NEG_INF = float("-inf")
VMEM_LIMIT = 52 * 1024 * 1024

OFF_GATE = 0
OFF_MQ, OFF_MK, OFF_MV = 3072, 3584, 4096
OFF_RW = 4608
OFF_DQ, OFF_DC, OFF_DIQ, OFF_DIKW = 6656, 7168, 7424, 7680
N_PROJ = 7808
RW_W = 2048


def _nt(a, b, precision=None):
    return lax.dot_general(a, b, (((1,), (1,)), ((), ())), precision=precision,
                           preferred_element_type=F32)


def _tn(a, b, precision=None):
    return lax.dot_general(a, b, (((0,), (0,)), ((), ())), precision=precision,
                           preferred_element_type=F32)


def _mm(a, b, precision=None):
    return jnp.dot(a, b, precision=precision, preferred_element_type=F32)


def _group_sum(t, gmat):
    hi = t.astype(BF16)
    lo = (t - hi.astype(F32)).astype(BF16)
    return _mm(hi, gmat) + _mm(lo, gmat)


def _head_rms(t, gain, gmat):
    ms = _group_sum(t * t, gmat) * (1.0 / HEAD_DIM)
    return t * lax.rsqrt(ms + NORM_EPS) * gain


def _resident(shape):
    nd = len(shape)
    return pl.BlockSpec(shape, lambda *_: (0,) * nd, pipeline_mode=pl.Buffered(1))


def _proj_kernel(x_ref, g_ref, w_ref, kvn_ref, kvup_ref, qkn_ref, gmat_ref,
                 gate_o, mq_o, mk_o, mv_o, rw_o, dq_o, dk_o, dv_o, diq_o, dikw_o):
    x = x_ref[...]
    ms = jnp.mean(x * x, axis=-1, keepdims=True)
    h = (x * lax.rsqrt(ms + NORM_EPS) * g_ref[...]).astype(BF16)
    gmat = gmat_ref[...]

    def proj(off, width):
        return _mm(h, w_ref[:, off:off + width])

    gate_o[...] = proj(OFF_GATE, 3 * D_MODEL).astype(BF16)
    mq_o[...] = _head_rms(proj(OFF_MQ, BRANCH_W), qkn_ref[0:1, :], gmat).astype(BF16)
    mk_o[...] = _head_rms(proj(OFF_MK, BRANCH_W), qkn_ref[1:2, :], gmat).astype(BF16)
    mv_o[...] = proj(OFF_MV, BRANCH_W).astype(BF16)
    rw_o[...] = proj(OFF_RW, RW_W)
    dq_o[...] = _head_rms(proj(OFF_DQ, BRANCH_W), qkn_ref[2:3, :], gmat).astype(BF16)
    dc = proj(OFF_DC, DSA_KV_RANK)
    dcn = dc * lax.rsqrt(jnp.mean(dc * dc, axis=-1, keepdims=True) + NORM_EPS) * kvn_ref[...]
    kv = _mm(dcn.astype(BF16), kvup_ref[...])
    dk_o[...] = _head_rms(kv[:, :BRANCH_W], qkn_ref[3:4, :], gmat).astype(BF16)
    dv_o[...] = kv[:, BRANCH_W:].astype(BF16)
    diq_o[...] = proj(OFF_DIQ, IDX_HEADS * IDX_DIM)
    dikw_o[...] = proj(OFF_DIKW, 128)


def _proj_call(x2, g, w_perm, kvn, kvup, qkn, gmat, tm=256):
    T = x2.shape[0]
    row = lambda w: pl.BlockSpec((tm, w), lambda i: (i, 0))
    outs = [
        ((T, 3 * D_MODEL), BF16), ((T, BRANCH_W), BF16), ((T, BRANCH_W), BF16), ((T, BRANCH_W), BF16),
        ((T, RW_W), F32), ((T, BRANCH_W), BF16), ((T, BRANCH_W), BF16), ((T, BRANCH_W), BF16),
        ((T, IDX_HEADS * IDX_DIM), F32), ((T, 128), F32),
    ]
    return pl.pallas_call(
        _proj_kernel,
        grid=(T // tm,),
        in_specs=[row(D_MODEL), _resident(g.shape), _resident(w_perm.shape), _resident(kvn.shape),
                  _resident(kvup.shape), _resident(qkn.shape), _resident(gmat.shape)],
        out_specs=[row(s[1]) for s, _ in outs],
        out_shape=[jax.ShapeDtypeStruct(s, d) for s, d in outs],
        compiler_params=pltpu.CompilerParams(dimension_semantics=("arbitrary",),
                                             vmem_limit_bytes=VMEM_LIMIT),
        name="proj",
    )(x2, g, w_perm, kvn, kvup, qkn, gmat)


def _flash_step(q, kj, vj, logit_add, m, l, acc):
    s = _nt(q, kj) * (HEAD_DIM ** -0.5) + logit_add
    m_new = jnp.maximum(m, jnp.max(s, axis=-1, keepdims=True))
    alpha = jnp.exp(m - m_new)
    p = jnp.exp(s - m_new)
    l = alpha * l + jnp.sum(p, axis=-1, keepdims=True)
    acc = alpha * acc + _mm(p.astype(BF16), vj)
    return m_new, l, acc


def _tile_bias(bias_ref, h, j, i):
    own = bias_ref[h, :, ATT_BLOCK:]
    prev = bias_ref[h, :, :ATT_BLOCK]
    far = bias_ref[h, ATT_BLOCK - 1:ATT_BLOCK, 0:1]
    return jnp.where(j == i, own, jnp.where(j == i - 1, prev, far))


def _moba_kernel(q_ref, k_ref, v_ref, bias_ref, o_ref, kmean_ref, *, nb):
    i = pl.program_id(2)

    @pl.when(i == 0)
    def _():
        for j in range(nb):
            blk = k_ref[j * ATT_BLOCK:(j + 1) * ATT_BLOCK, :].astype(F32)
            kmean_ref[j:j + 1, :] = jnp.mean(blk, axis=0, keepdims=True)

    col = lax.broadcasted_iota(jnp.int32, (ATT_BLOCK, nb), 1)
    n_sel = min(MOBA_TOPK, nb - 1)
    outs = []
    for hh in range(2):
        lanes = slice(hh * HEAD_DIM, (hh + 1) * HEAD_DIM)
        q = q_ref[:, lanes]
        gate = _nt(q.astype(F32), kmean_ref[:, lanes], precision=HIGHEST)
        past = col < i
        rank = jnp.zeros((ATT_BLOCK, nb), F32)
        for jp in range(nb):
            gj = gate[:, jp:jp + 1]
            ahead = (gj > gate) | ((gj == gate) & (jp < col))
            rank = rank + jnp.where(ahead & (jp < i), 1.0, 0.0)
        sel = jnp.where(past & (rank < n_sel), 1.0, 0.0)

        def body(j, carry, q=q, sel=sel, lanes=lanes, hh=hh):
            m, l, acc = carry
            rows = pl.ds(pl.multiple_of(j * ATT_BLOCK, ATT_BLOCK), ATT_BLOCK)
            kj = k_ref[rows, lanes]
            vj = v_ref[rows, lanes]
            chosen = jnp.sum(jnp.where(col == j, sel, 0.0), axis=-1, keepdims=True) > 0.0
            add = jnp.where(chosen | (j == i), _tile_bias(bias_ref, hh, j, i), NEG_INF)
            return _flash_step(q, kj, vj, add, m, l, acc)

        init = (jnp.full((ATT_BLOCK, 1), -1e30, F32), jnp.zeros((ATT_BLOCK, 1), F32),
                jnp.zeros((ATT_BLOCK, HEAD_DIM), F32))
        m, l, acc = lax.fori_loop(0, i + 1, body, init)
        outs.append(acc / l)
    o_ref[...] = jnp.concatenate(outs, axis=-1).astype(o_ref.dtype)


def _moba_call(mq, mk, mv, bias, B, S):
    nb = S // ATT_BLOCK
    T = B * S
    qspec = pl.BlockSpec((ATT_BLOCK, 128), lambda b, g, i: (b * nb + i, g))
    kvspec = pl.BlockSpec((S, 128), lambda b, g, i: (b, g))
    return pl.pallas_call(
        functools.partial(_moba_kernel, nb=nb),
        grid=(B, N_HEADS // 2, nb),
        in_specs=[qspec, kvspec, kvspec,
                  pl.BlockSpec((2, ATT_BLOCK, 2 * ATT_BLOCK), lambda b, g, i: (g, 0, 0))],
        out_specs=qspec,
        out_shape=jax.ShapeDtypeStruct((T, BRANCH_W), BF16),
        scratch_shapes=[pltpu.VMEM((nb, 128), F32)],
        compiler_params=pltpu.CompilerParams(
            dimension_semantics=("arbitrary", "arbitrary", "arbitrary"),
            vmem_limit_bytes=VMEM_LIMIT),
        name="moba",
    )(mq, mk, mv, bias)


def _dsa_kernel(q_ref, k_ref, v_ref, qi_ref, wq_ref, kx_ref, bias_ref, o_ref,
                keys_ref, mask_ref, *, n_keep, n_idx_bits):
    i = pl.program_id(1)
    Q = ATT_BLOCK
    row = lax.broadcasted_iota(jnp.int32, (Q, Q), 0)
    colt = lax.broadcasted_iota(jnp.int32, (Q, Q), 1)

    def blk(j):
        return pl.ds(pl.multiple_of(j * Q, Q), Q)

    def score_body(j, _):
        kx = kx_ref[blk(j), 0:IDX_DIM]
        acc = jnp.zeros((Q, Q), F32)
        for h in range(IDX_HEADS):
            s_h = _nt(qi_ref[:, h * IDX_DIM:(h + 1) * IDX_DIM], kx, precision=HIGHEST)
            acc = acc + jnp.maximum(s_h, 0.0) * wq_ref[:, IDX_DIM + h:IDX_DIM + h + 1]
        acc = jnp.where(acc == 0.0, 0.0, acc)
        bits = pltpu.bitcast(acc, jnp.int32)
        key = jnp.where(bits < 0, bits ^ jnp.int32(0x7FFFFFFF), bits)
        causal = (j * Q + colt) <= (i * Q + row)
        keys_ref[:, blk(j)] = jnp.where(causal, key, jnp.int32(INT_MIN))
        return 0

    lax.fori_loop(0, i + 1, score_body, 0)

    def count(pred):
        def body(j, acc):
            hit = jnp.where(pred(keys_ref[:, blk(j)], j * Q), 1.0, 0.0)
            return acc + hit[:, :128] + hit[:, 128:]
        acc = lax.fori_loop(0, i + 1, body, jnp.zeros((Q, 128), F32))
        return jnp.sum(acc, axis=-1, keepdims=True)

    def thr_body(it, thr):
        cand = thr + jnp.left_shift(jnp.int32(1), 31 - it)
        cnt = count(lambda kb, c0: kb >= cand)
        return jnp.where(cnt >= n_keep, cand, thr)

    thr = lax.fori_loop(0, 32, thr_body, jnp.full((Q, 1), INT_MIN, jnp.int32))

    need = n_keep - count(lambda kb, c0: kb > thr)

    def cut_body(it, cut):
        cand = cut + jnp.left_shift(jnp.int32(1), n_idx_bits - 1 - it)
        cnt = count(lambda kb, c0: (kb == thr) & ((c0 + colt) < cand))
        return jnp.where(cnt <= need, cand, cut)

    cut = lax.fori_loop(0, n_idx_bits, cut_body, jnp.zeros((Q, 1), jnp.int32))

    def mask_body(j, _):
        kb = keys_ref[:, blk(j)]
        keep = (kb > thr) | ((kb == thr) & ((j * Q + colt) < cut) & (kb > jnp.int32(INT_MIN)))
        mask_ref[:, blk(j)] = jnp.where(keep, 0.0, NEG_INF)
        return 0

    lax.fori_loop(0, i + 1, mask_body, 0)

    for h in range(N_HEADS):
        lanes = slice(h * HEAD_DIM, (h + 1) * HEAD_DIM)
        q = q_ref[:, lanes]

        def body(j, carry, q=q, lanes=lanes, h=h):
            m, l, acc = carry
            add = mask_ref[:, blk(j)] + _tile_bias(bias_ref, h, j, i)
            return _flash_step(q, k_ref[blk(j), lanes], v_ref[blk(j), lanes], add, m, l, acc)

        init = (jnp.full((Q, 1), -1e30, F32), jnp.zeros((Q, 1), F32), jnp.zeros((Q, HEAD_DIM), F32))
        m, l, acc = lax.fori_loop(0, i + 1, body, init)
        o_ref[:, lanes] = (acc / l).astype(o_ref.dtype)


def _dsa_call(dq, dk, dv, diq, dikw, bias, B, S):
    nq = S // ATT_BLOCK
    T = B * S
    n_keep = min(DSA_TOPK_MAX, S // 4)
    qrow = lambda w: pl.BlockSpec((ATT_BLOCK, w), lambda b, i: (b * nq + i, 0))
    seq = lambda w: pl.BlockSpec((S, w), lambda b, i: (b, 0))
    return pl.pallas_call(
        functools.partial(_dsa_kernel, n_keep=n_keep, n_idx_bits=int(math.log2(S)) + 1),
        grid=(B, nq),
        in_specs=[qrow(BRANCH_W), seq(BRANCH_W), seq(BRANCH_W), qrow(IDX_HEADS * IDX_DIM),
                  qrow(128), seq(128), _resident(bias.shape)],
        out_specs=qrow(BRANCH_W),
        out_shape=jax.ShapeDtypeStruct((T, BRANCH_W), BF16),
        scratch_shapes=[pltpu.VMEM((ATT_BLOCK, S), jnp.int32), pltpu.VMEM((ATT_BLOCK, S), F32)],
        compiler_params=pltpu.CompilerParams(dimension_semantics=("arbitrary", "arbitrary"),
                                             vmem_limit_bytes=VMEM_LIMIT),
        name="dsa",
    )(dq, dk, dv, diq, dikw, dikw, bias)


def _rwkv_pre_kernel(*refs, tiles_per_seq, has_vres):
    if has_vres:
        (rw_ref, prev_ref, mu_ref, vec_ref, w2_ref, a2_ref, g2_ref, gmat_ref,
         vf_ref, v0_ref, va_ref, vb_ref, r_o, lw_o, k_o, v_o, a_o, b_o, g_o) = refs
    else:
        (rw_ref, prev_ref, mu_ref, vec_ref, w2_ref, a2_ref, g2_ref, gmat_ref,
         r_o, lw_o, k_o, v_o, a_o, b_o, g_o) = refs
    i = pl.program_id(0)
    p = rw_ref[...]
    tm = p.shape[0]
    last = prev_ref[7:8, :]
    last = jnp.where(i % tiles_per_seq == 0, 0.0, last)
    rid = lax.broadcasted_iota(jnp.int32, p.shape, 0)
    shifted = jnp.where(rid == 0, last, pltpu.roll(p, 1, 0))
    p = p + (shifted - p) * mu_ref[...]
    r = p[:, 0:512]
    k = p[:, 512:1024]
    v = p[:, 1024:1536]
    wd = p[:, 1536:1536 + LORA_DECAY]
    ad = p[:, 1664:1664 + LORA_AAA]
    gd = p[:, 1792:2048]
    w0, a0, k_k, k_a = vec_ref[0:1, :], vec_ref[1:2, :], vec_ref[2:3, :], vec_ref[3:4, :]
    z = -(w0 + _mm(jnp.tanh(wd), w2_ref[...], HIGHEST))
    softplus = jnp.maximum(z, 0.0) + jnp.log(1.0 + jnp.exp(-jnp.abs(z)))
    w = -softplus - 0.5
    lw_o[...] = -jnp.exp(w)
    a = jax.nn.sigmoid(a0 + _mm(ad, a2_ref[...], HIGHEST))
    g_o[...] = _mm(jax.nn.sigmoid(gd), g2_ref[...], HIGHEST)
    if has_vres:
        mix = jax.nn.sigmoid(v0_ref[...] + _mm(_mm(v, va_ref[...], HIGHEST), vb_ref[...], HIGHEST))
        v = v + (vf_ref[...] - v) * mix
    kk = k * k_k
    nrm = jnp.sqrt(_group_sum(kk * kk, gmat_ref[...]))
    kk = kk / jnp.maximum(nrm, 1e-12)
    r_o[...] = r
    k_o[...] = k * (1.0 + (a - 1.0) * k_a)
    v_o[...] = v
    a_o[...] = -kk
    b_o[...] = kk * a


def _rwkv_pre_call(rw, mu, vec, w2, a2, g2, gmat, S, vres=None, tm=256):
    T = rw.shape[0]
    tiles_per_seq = S // tm
    row = lambda w: pl.BlockSpec((tm, w), lambda i: (i, 0))
    prev = pl.BlockSpec((8, RW_W), lambda i: (jnp.maximum(i * (tm // 8) - 1, 0), 0))
    ins = [rw, rw, mu, vec, w2, a2, g2, gmat]
    specs = [row(RW_W), prev] + [_resident(t.shape) for t in ins[2:]]
    if vres is not None:
        vf, v0, va, vb = vres
        ins += [vf, v0, va, vb]
        specs += [row(BRANCH_W), _resident(v0.shape), _resident(va.shape), _resident(vb.shape)]
    return pl.pallas_call(
        functools.partial(_rwkv_pre_kernel, tiles_per_seq=tiles_per_seq, has_vres=vres is not None),
        grid=(T // tm,),
        in_specs=specs,
        out_specs=[row(BRANCH_W)] * 7,
        out_shape=[jax.ShapeDtypeStruct((T, BRANCH_W), F32)] * 7,
        compiler_params=pltpu.CompilerParams(dimension_semantics=("arbitrary",),
                                             vmem_limit_bytes=VMEM_LIMIT),
        name="rwkv_pre",
    )(*ins)


def _rwkv_scan_kernel(r_ref, lw_ref, k_ref, v_ref, a_ref, b_ref, g_ref, vec_ref, gmat_ref, tri_ref,
                      o_ref, state_ref, oraw_ref, *, n_chunks):
    C = RWKV_CHUNK
    N = HEAD_DIM
    bf = lambda t: t.astype(BF16)

    @pl.when(pl.program_id(1) == 0)
    def _():
        state_ref[...] = jnp.zeros_like(state_ref)

    rr = lax.broadcasted_iota(jnp.int32, (C, C), 0)
    cc = lax.broadcasted_iota(jnp.int32, (C, C), 1)
    strict = rr > cc
    incl = rr >= cc
    eye = jnp.where(rr == cc, 1.0, 0.0)

    lw = lw_ref[...]
    cum = _mm(tri_ref[...], lw, HIGHEST)
    r, k, v, a, b = r_ref[...], k_ref[...], v_ref[...], a_ref[...], b_ref[...]
    e_neg = jnp.exp(-cum)
    at = a * jnp.exp(cum - lw)
    rt = r * jnp.exp(cum)
    bt = b * e_neg
    kt = k * e_neg
    bp, kp, p_end = [], [], []
    for c in range(n_chunks):
        rows = slice(c * C, (c + 1) * C)
        cum_last = cum[(c + 1) * C - 1:(c + 1) * C, :]
        e_end = jnp.exp(cum_last - cum[rows])
        bp.append(b[rows] * e_end)
        kp.append(k[rows] * e_end)
        p_end.append(jnp.exp(cum_last))

    pairs = [(c, h) for c in range(n_chunks) for h in range(N_HEADS)]
    sl = lambda t, c, h: t[c * C:(c + 1) * C, h * N:(h + 1) * N]
    hl = lambda t, h: t[:, h * N:(h + 1) * N]
    each = lambda f: {p: f(*p) for p in pairs}

    aall = each(lambda c, h: _nt(bf(jnp.concatenate([sl(at, c, h), sl(rt, c, h)], axis=0)),
                                 bf(jnp.concatenate([sl(bt, c, h), sl(kt, c, h)], axis=0))))
    vb = each(lambda c, h: bf(sl(v, c, h)))
    akv = each(lambda c, h: _mm(bf(jnp.where(strict, aall[c, h][:C, C:], 0.0)), vb[c, h]))
    o_loc = each(lambda c, h: _mm(bf(jnp.where(incl, aall[c, h][C:, C:], 0.0)), vb[c, h]))
    a_rb = each(lambda c, h: bf(jnp.where(incl, aall[c, h][C:, :C], 0.0)))
    x = each(lambda c, h: jnp.where(strict, aall[c, h][:C, :C], 0.0))
    tinv = each(lambda c, h: eye + x[c, h])
    for _ in range(int(math.log2(C)) - 1):
        x = each(lambda c, h: _mm(bf(x[c, h]), bf(x[c, h])))
        tinv = each(lambda c, h: tinv[c, h] + _mm(bf(tinv[c, h]), bf(x[c, h])))
    w_t = each(lambda c, h: _mm(bf(tinv[c, h]), bf(sl(at, c, h))))
    u_loc = each(lambda c, h: _mm(bf(tinv[c, h]), bf(akv[c, h])))

    heads = range(N_HEADS)
    state = [state_ref[h] for h in heads]
    for c in range(n_chunks):
        xs = [_nt(bf(jnp.concatenate([w_t[c, h], sl(rt, c, h)], axis=0)), bf(state[h])) for h in heads]
        u = [xs[h][:C] + u_loc[c, h] for h in heads]
        upd = [_tn(bf(jnp.concatenate([u[h], sl(v, c, h)], axis=0)),
                   bf(jnp.concatenate([hl(bp[c], h), hl(kp[c], h)], axis=0))) for h in heads]
        state = [state[h] * hl(p_end[c], h) + upd[h] for h in heads]
        o = [xs[h][C:] + _mm(a_rb[c, h], bf(u[h])) + o_loc[c, h] for h in heads]
        for h in heads:
            oraw_ref[c * C:(c + 1) * C, h * N:(h + 1) * N] = o[h]
    for h in heads:
        state_ref[h] = state[h]

    gmat = gmat_ref[...]
    o = oraw_ref[...]
    ln_w, ln_b, r_k = vec_ref[0:1, :], vec_ref[1:2, :], vec_ref[2:3, :]
    mean = _group_sum(o, gmat) * (1.0 / N)
    d = o - mean
    var = _group_sum(d * d, gmat) * (1.0 / N)
    o = d * lax.rsqrt(var + RWKV_GN_EPS) * ln_w + ln_b
    bonus = _group_sum(r * k * r_k, gmat)
    o_ref[...] = ((o + bonus * v) * g_ref[...]).astype(o_ref.dtype)


def _chunk_cumsum_matrix(tt):
    t = np.arange(tt)
    same_chunk = (t[:, None] // RWKV_CHUNK) == (t[None, :] // RWKV_CHUNK)
    return (same_chunk & (t[:, None] >= t[None, :])).astype(np.float32)


def _rwkv_scan_call(r, lw, k, v, a, b, g, vec, gmat, tri, B, S, tt=RWKV_TILE):
    T = B * S
    nt = S // tt
    row = pl.BlockSpec((tt, BRANCH_W), lambda bi, t: (bi * nt + t, 0))
    return pl.pallas_call(
        functools.partial(_rwkv_scan_kernel, n_chunks=tt // RWKV_CHUNK),
        grid=(B, nt),
        in_specs=[row] * 7 + [_resident(vec.shape), _resident(gmat.shape), _resident(tri.shape)],
        out_specs=row,
        out_shape=jax.ShapeDtypeStruct((T, BRANCH_W), BF16),
        scratch_shapes=[pltpu.VMEM((N_HEADS, HEAD_DIM, HEAD_DIM), F32),
                        pltpu.VMEM((tt, BRANCH_W), F32)],
        compiler_params=pltpu.CompilerParams(dimension_semantics=("arbitrary", "arbitrary"),
                                             vmem_limit_bytes=VMEM_LIMIT),
        name="rwkv_scan",
    )(r, lw, k, v, a, b, g, vec, gmat, tri)


def _merge_ffn_kernel(x_ref, gate_ref, oa_ref, ob_ref, oc_ref, wbr_ref, wo_ref, nf_ref,
                      wfi_ref, wfo_ref, o_ref, act_ref, *, ff_chunk):
    mix = None
    for n, br in enumerate((oa_ref, ob_ref, oc_ref)):
        y = _mm(br[...], wbr_ref[n])
        gt = jax.nn.sigmoid(gate_ref[:, n * D_MODEL:(n + 1) * D_MODEL].astype(F32))
        mix = gt * y if mix is None else mix + gt * y
    x = x_ref[...] + _mm(mix.astype(BF16), wo_ref[...])
    ms = jnp.mean(x * x, axis=-1, keepdims=True)
    h = (x * lax.rsqrt(ms + NORM_EPS) * nf_ref[...]).astype(BF16)
    for c in range(0, D_FF, ff_chunk):
        fg = _mm(h, wfi_ref[:, c:c + ff_chunk])
        fu = _mm(h, wfi_ref[:, D_FF + c:D_FF + c + ff_chunk])
        act_ref[:, c:c + ff_chunk] = (fg * jax.nn.sigmoid(fg) * fu).astype(BF16)
    o_ref[...] = x + _mm(act_ref[...], wfo_ref[...])


def _merge_ffn_call(x2, gate, oa, ob, oc, wbr, wo, nf, wfi, wfo, tm=512, ff_chunk=256):
    T = x2.shape[0]
    row = lambda w: pl.BlockSpec((tm, w), lambda i: (i, 0))
    return pl.pallas_call(
        functools.partial(_merge_ffn_kernel, ff_chunk=ff_chunk),
        grid=(T // tm,),
        in_specs=[row(D_MODEL), row(3 * D_MODEL), row(BRANCH_W), row(BRANCH_W), row(BRANCH_W),
                  _resident(wbr.shape), _resident(wo.shape), _resident(nf.shape),
                  _resident(wfi.shape), _resident(wfo.shape)],
        out_specs=row(D_MODEL),
        out_shape=jax.ShapeDtypeStruct((T, D_MODEL), F32),
        scratch_shapes=[pltpu.VMEM((tm, D_FF), BF16)],
        compiler_params=pltpu.CompilerParams(dimension_semantics=("arbitrary",),
                                             vmem_limit_bytes=VMEM_LIMIT),
        name="merge_ffn",
    )(x2, gate, oa, ob, oc, wbr, wo, nf, wfi, wfo)


def _bucket_table():
    n = np.arange(2 * ATT_BLOCK)
    max_exact = REL_BUCKETS // 2
    nf = np.maximum(n, 1).astype(np.float32)
    large = max_exact + (np.log(nf / max_exact) / math.log(REL_MAX_DIST / max_exact)
                         * (REL_BUCKETS - max_exact)).astype(np.int32)
    large = np.minimum(large, REL_BUCKETS - 1)
    return np.where(n < max_exact, n, large).astype(np.int32)


def _bias_strips(rel_bias):
    r = np.arange(ATT_BLOCK)[:, None]
    c = np.arange(2 * ATT_BLOCK)[None, :]
    dist = np.where(c < ATT_BLOCK, ATT_BLOCK + r - c, r - (c - ATT_BLOCK))
    bucket = _bucket_table()[np.maximum(dist, 0)]
    strips = jnp.transpose(rel_bias[bucket], (2, 0, 1))
    return jnp.where(jnp.asarray(dist >= 0)[None], strips, NEG_INF)


def _pad_cols(t, width):
    return jnp.pad(t, ((0, 0), (0, width - t.shape[1])))


def _pad_rows(t, height):
    return jnp.pad(t, ((0, height - t.shape[0]), (0, 0)))


def _permute_w_in(w):
    o = 0
    def take(n):
        nonlocal o
        s = w[:, o:o + n]
        o += n
        return s
    mq, mk, mv = take(512), take(512), take(512)
    r, wd, k, v, ad, gd = take(512), take(LORA_DECAY), take(512), take(512), take(LORA_AAA), take(LORA_GATE)
    dq, dc, diq, dik, diw = take(512), take(DSA_KV_RANK), take(256), take(IDX_DIM), take(IDX_HEADS)
    gate = take(3 * D_MODEL)
    pieces = [gate, mq, mk, mv, r, k, v, _pad_cols(wd, 128), _pad_cols(ad, 128), _pad_cols(gd, 256),
              dq, dc, diq, _pad_cols(jnp.concatenate([dik, diw], axis=1), 128)]
    return jnp.concatenate(pieces, axis=1).astype(BF16)


def _permute_mu(mu):
    r, wd, k, v, ad, gd = jnp.split(mu, np.cumsum([512, LORA_DECAY, 512, 512, LORA_AAA])[:].tolist())
    z = lambda n: jnp.zeros((n,), F32)
    return jnp.concatenate([r, k, v, wd, z(64), ad, z(64), gd, z(256 - LORA_GATE)])[None, :]


def kernel(x, w_in, norm_mix, norm_ffn, qk_norm, rel_bias, rwkv_mu, rwkv_w0, rwkv_w2, rwkv_a0, rwkv_a2, rwkv_g2, rwkv_kk, rwkv_ka, rwkv_rk, rwkv_ln_w, rwkv_ln_b, rwkv_v0, rwkv_va, rwkv_vb, dsa_kv_norm, dsa_kv_up, w_branch, w_o, w_ffn_in, w_ffn_out):
    B, S, D = x.shape
    depth = w_in.shape[0]
    T = B * S
    strips = _bias_strips(rel_bias)
    moba_bias, dsa_bias = strips[:N_HEADS], strips[N_HEADS:]
    head_of_lane = np.arange(BRANCH_W) // HEAD_DIM
    gmat = jnp.asarray(head_of_lane[:, None] == head_of_lane[None, :], BF16)
    tri = jnp.asarray(_chunk_cumsum_matrix(RWKV_TILE))
    x2 = x.reshape(T, D)
    v_first = None
    for l in range(depth):
        qkn = jnp.tile(qk_norm[l], (1, N_HEADS))
        (gate, mq, mk, mv, rw, dq, dk, dv, diq, dikw) = _proj_call(
            x2, norm_mix[l][None, :], _permute_w_in(w_in[l]), dsa_kv_norm[l][None, :],
            dsa_kv_up[l].astype(BF16), qkn, gmat)
        o_moba = _moba_call(mq, mk, mv, moba_bias, B, S)
        o_dsa = _dsa_call(dq, dk, dv, diq, dikw, dsa_bias, B, S)
        vec = jnp.stack([rwkv_w0[l], rwkv_a0[l], rwkv_kk[l], rwkv_ka[l]])
        vres = None
        if l > 0:
            vres = (v_first, rwkv_v0[l - 1][None, :], _pad_cols(rwkv_va[l - 1], 128),
                    _pad_rows(rwkv_vb[l - 1], 128))
        r, lw, k, v, a, b, g = _rwkv_pre_call(
            rw, _permute_mu(rwkv_mu[l]), vec, rwkv_w2[l], rwkv_a2[l], _pad_rows(rwkv_g2[l], 256),
            gmat, S, vres)
        if l == 0:
            v_first = v
        vec2 = jnp.stack([rwkv_ln_w[l], rwkv_ln_b[l], rwkv_rk[l].reshape(-1)])
        o_rwkv = _rwkv_scan_call(r, lw, k, v, a, b, g, vec2, gmat, tri, B, S)
        x2 = _merge_ffn_call(x2, gate, o_moba, o_rwkv, o_dsa, w_branch[l].astype(BF16),
                             w_o[l].astype(BF16), norm_ffn[l][None, :],
                             w_ffn_in[l].astype(BF16), w_ffn_out[l].astype(BF16))
    return x2.reshape(B, S, D)
```

```python
import functools
import math

import numpy as np
import jax
import jax.numpy as jnp
from jax import lax
from jax.experimental import pallas as pl
from jax.experimental.pallas import tpu as pltpu

F32 = jnp.float32
BF16 = jnp.bfloat16
HIGHEST = lax.Precision.HIGHEST

D_MODEL = 1024
HEAD_DIM = 64
BRANCH_W = 512
N_HEADS = BRANCH_W // HEAD_DIM
MOBA_BLOCK = 256
MOBA_TOPK = 3
LORA_DECAY = 64
LORA_AAA = 64
LORA_GATE = 160
LORA_MV = 32
RWKV_GN_EPS = 64e-5
DSA_KV_RANK = 256
IDX_HEADS = 8
IDX_DIM = 32
DSA_TOPK_MAX = 256
REL_BUCKETS = 32
REL_MAX_DIST = 128
D_FF = 2816
NORM_EPS = 1e-6

ATT_BLOCK = 256
RWKV_CHUNK = 64
RWKV_TILE = 256
INT_MIN = -(2 ** 31)
NEG_INF = float("-inf")
LOG2E = math.log2(math.e)
VMEM_LIMIT = 52 * 1024 * 1024

TM_GATE, TM_MK, TM_RW, TM_DC, TM_DIK, N_TM = 0, 3072, 3584, 5632, 5888, 6016
CM_MQ, CM_MV, CM_DQ, CM_DIQ, CM_DIW, N_CM = 0, 512, 1024, 1536, 1792, 1800
RW_W = 2048


def _nt(a, b, precision=None):
    return lax.dot_general(a, b, (((1,), (1,)), ((), ())), precision=precision,
                           preferred_element_type=F32)


def _tn(a, b, precision=None):
    return lax.dot_general(a, b, (((0,), (0,)), ((), ())), precision=precision,
                           preferred_element_type=F32)


def _mm(a, b, precision=None):
    return jnp.dot(a, b, precision=precision, preferred_element_type=F32)


def _group_sum(t, gmat):
    hi = t.astype(BF16)
    lo = (t - hi.astype(F32)).astype(BF16)
    return _mm(hi, gmat) + _mm(lo, gmat)


def _head_rms(t, gain, gmat):
    ms = _group_sum(t * t, gmat) * (1.0 / HEAD_DIM)
    return t * lax.rsqrt(ms + NORM_EPS) * gain


def _resident(shape):
    nd = len(shape)
    return pl.BlockSpec(shape, lambda *_: (0,) * nd, pipeline_mode=pl.Buffered(1))


def _head_rms_cm(t, gain):
    parts = []
    for h in range(N_HEADS):
        blk = t[h * HEAD_DIM:(h + 1) * HEAD_DIM, :]
        ms = jnp.mean(blk * blk, axis=0, keepdims=True)
        parts.append(blk * lax.rsqrt(ms + NORM_EPS))
    return jnp.concatenate(parts, axis=0) * gain


def _proj_kernel(x_ref, g_ref, wtm_ref, wcm_ref, kvn_ref, kvk_ref, kvvt_ref, qkn_ref, gq_ref, gmat_ref,
                 gate_o, mqt_o, mk_o, mvt_o, rw_o, dqt_o, dk_o, dvt_o, diqt_o, dik_o, diwt_o):
    x = x_ref[...]
    ms = jnp.mean(x * x, axis=-1, keepdims=True)
    h = (x * lax.rsqrt(ms + NORM_EPS) * g_ref[...]).astype(BF16)
    gmat = gmat_ref[...]

    def proj(off, width):
        return _mm(h, wtm_ref[:, off:off + width])

    def proj_t(off, width):
        return _nt(wcm_ref[off:off + width, :], h)

    gate_o[...] = proj(TM_GATE, 3 * D_MODEL).astype(BF16)
    mqt_o[...] = _head_rms_cm(proj_t(CM_MQ, BRANCH_W), gq_ref[0]).astype(BF16)
    mk_o[...] = _head_rms(proj(TM_MK, BRANCH_W), qkn_ref[0:1, :], gmat).astype(BF16)
    mvt_o[...] = proj_t(CM_MV, BRANCH_W).astype(BF16)
    rw_o[...] = proj(TM_RW, RW_W)
    dqt_o[...] = _head_rms_cm(proj_t(CM_DQ, BRANCH_W), gq_ref[1]).astype(BF16)
    dc = proj(TM_DC, DSA_KV_RANK)
    dcn = (dc * lax.rsqrt(jnp.mean(dc * dc, axis=-1, keepdims=True) + NORM_EPS) * kvn_ref[...]).astype(BF16)
    dk_o[...] = _head_rms(_mm(dcn, kvk_ref[...]), qkn_ref[1:2, :], gmat).astype(BF16)
    dvt_o[...] = _nt(kvvt_ref[...], dcn).astype(BF16)
    diqt_o[...] = proj_t(CM_DIQ, IDX_HEADS * IDX_DIM)
    dik_o[...] = proj(TM_DIK, 128)
    diwt_o[...] = proj_t(CM_DIW, IDX_HEADS)


def _proj_call(x2, g, w_tm, w_cm, kvn, kvk, kvvt, qkn, gq, gmat, tm=ATT_BLOCK):
    T = x2.shape[0]
    row = lambda w: pl.BlockSpec((tm, w), lambda i: (i, 0))
    col = lambda c: pl.BlockSpec((c, tm), lambda i: (0, i))
    tmaj = lambda w, d: (jax.ShapeDtypeStruct((T, w), d), row(w))
    cmaj = lambda c, d: (jax.ShapeDtypeStruct((c, T), d), col(c))
    outs = [tmaj(3 * D_MODEL, BF16), cmaj(BRANCH_W, BF16), tmaj(BRANCH_W, BF16), cmaj(BRANCH_W, BF16),
            tmaj(RW_W, F32), cmaj(BRANCH_W, BF16), tmaj(BRANCH_W, BF16), cmaj(BRANCH_W, BF16),
            cmaj(IDX_HEADS * IDX_DIM, F32), tmaj(128, F32), cmaj(IDX_HEADS, F32)]
    ins = [x2, g, w_tm, w_cm, kvn, kvk, kvvt, qkn, gq, gmat]
    return pl.pallas_call(
        _proj_kernel,
        grid=(T // tm,),
        in_specs=[row(D_MODEL)] + [_resident(t.shape) for t in ins[1:]],
        out_specs=[spec for _, spec in outs],
        out_shape=[shape for shape, _ in outs],
        compiler_params=pltpu.CompilerParams(dimension_semantics=("arbitrary",),
                                             vmem_limit_bytes=VMEM_LIMIT),
        name="proj",
    )(*ins)


ONES_ROWS = 16


def _masked_attention(i, qt_ref, k_ref, vt_ref, bias_ref, o_ref, mask_fn):
    Q = ATT_BLOCK
    H = range(N_HEADS)

    def blk(j):
        return pl.ds(pl.multiple_of(j * Q, Q), Q)

    prow = lax.broadcasted_iota(jnp.int32, (2 * HEAD_DIM, Q), 0)
    qm = []
    for p in range(N_HEADS // 2):
        pair = qt_ref[p * 2 * HEAD_DIM:(p + 1) * 2 * HEAD_DIM, :]
        zero = jnp.zeros_like(pair)
        qm += [jnp.where(prow < HEAD_DIM, pair, zero), jnp.where(prow >= HEAD_DIM, pair, zero)]
    ones = jnp.ones((ONES_ROWS, Q), BF16)

    def tile(j, carry, kind):
        kt = [k_ref[blk(j), p * 2 * HEAD_DIM:(p + 1) * 2 * HEAD_DIM] for p in range(N_HEADS // 2)]
        st = [_mm(kt[h // 2], qm[h]) for h in H]
        if kind == "prev":
            st = [st[h] + bias_ref[h, :Q, :] for h in H]
        elif kind == "own":
            st = [st[h] + bias_ref[h, Q:, :] for h in H]
        st = [mask_fn(h, j, st[h], kind) for h in H]
        tmax = [jnp.max(st[h], axis=0, keepdims=True) for h in H]
        if kind == "far":
            far = [bias_ref[h, 0:1, Q - 1:Q] for h in H]
            tmax = [tmax[h] + far[h] for h in H]
        m_new = [jnp.maximum(carry[h][0], tmax[h]) for h in H]
        alpha = [jnp.exp2(carry[h][0] - m_new[h]) for h in H]
        sub = [m_new[h] - far[h] for h in H] if kind == "far" else m_new
        p = [jnp.exp2(st[h] - sub[h]).astype(BF16) for h in H]
        vt = [jnp.concatenate([vt_ref[h * HEAD_DIM:(h + 1) * HEAD_DIM, blk(j)], ones], axis=0) for h in H]
        pv = [_mm(vt[h], p[h]) for h in H]
        return tuple((m_new[h], alpha[h] * carry[h][1] + pv[h]) for h in H)

    init = tuple((jnp.full((1, Q), -1e30, F32), jnp.zeros((HEAD_DIM + ONES_ROWS, Q), F32)) for _ in H)
    carry = lax.fori_loop(0, i - 1, lambda j, c: tile(j, c, "far"), init)
    carry = tile(jnp.maximum(i - 1, 0), carry, "prev")
    carry = tile(i, carry, "own")
    for h in H:
        acc = carry[h][1]
        o_ref[h * HEAD_DIM:(h + 1) * HEAD_DIM, :] = (acc[:HEAD_DIM] / acc[HEAD_DIM:HEAD_DIM + 1]).astype(o_ref.dtype)


def _moba_kernel(qt_ref, k_ref, vt_ref, bias_ref, o_ref, kmean_ref, sel_ref, *, nb):
    i = pl.program_id(1)
    Q = ATT_BLOCK

    @pl.when(i == 0)
    def _():
        for j in range(nb):
            kb = k_ref[j * Q:(j + 1) * Q, :].astype(F32)
            kmean_ref[j:j + 1, :] = jnp.mean(kb, axis=0, keepdims=True)

    qt32 = qt_ref[...].astype(F32)
    lane_head = lax.broadcasted_iota(jnp.int32, (nb, BRANCH_W), 1) // HEAD_DIM
    brow = lax.broadcasted_iota(jnp.int32, (nb, Q), 0)
    n_sel = min(MOBA_TOPK, nb - 1)
    kmean = kmean_ref[...]
    gate = [_mm(jnp.where(lane_head == h, kmean, 0.0), qt32, HIGHEST) for h in range(N_HEADS)]
    for h in range(N_HEADS):
        rank = jnp.zeros((nb, Q), F32)
        for jp in range(nb):
            gj = gate[h][jp:jp + 1, :]
            ahead = (gj > gate[h]) | ((gj == gate[h]) & (jp < brow))
            rank = rank + jnp.where(ahead & (jp < i), 1.0, 0.0)
        sel_ref[h] = jnp.where((brow < i) & (rank < n_sel), 1.0, 0.0)

    def mask_fn(h, j, st, kind):
        if kind == "own":
            return st
        return jnp.where(sel_ref[h, pl.ds(j, 1), :] > 0.0, st, NEG_INF)

    _masked_attention(i, qt_ref, k_ref, vt_ref, bias_ref, o_ref, mask_fn)


def _moba_call(mqt, mk, mvt, bias_t, B, S):
    nb = S // ATT_BLOCK
    T = B * S
    qspec = pl.BlockSpec((BRANCH_W, ATT_BLOCK), lambda b, i: (0, b * nb + i))
    return pl.pallas_call(
        functools.partial(_moba_kernel, nb=nb),
        grid=(B, nb),
        in_specs=[qspec,
                  pl.BlockSpec((S, BRANCH_W), lambda b, i: (b, 0)),
                  pl.BlockSpec((BRANCH_W, S), lambda b, i: (0, b)),
                  _resident(bias_t.shape)],
        out_specs=qspec,
        out_shape=jax.ShapeDtypeStruct((BRANCH_W, T), BF16),
        scratch_shapes=[pltpu.VMEM((nb, BRANCH_W), F32), pltpu.VMEM((N_HEADS, nb, ATT_BLOCK), F32)],
        compiler_params=pltpu.CompilerParams(dimension_semantics=("arbitrary", "arbitrary"),
                                             vmem_limit_bytes=VMEM_LIMIT),
        name="moba",
    )(mqt, mk, mvt, bias_t)


def _dsa_kernel(qt_ref, k_ref, vt_ref, qit_ref, wt_ref, kx_ref, bias_ref, perm_ref, o_ref,
                kx3_ref, keys_ref, mask_ref, cut_ref, *, n_keep, n_idx_bits, nq):
    i = pl.program_id(1)
    Q = ATT_BLOCK
    krow = lax.broadcasted_iota(jnp.int32, (Q, Q), 0)
    qcol = lax.broadcasted_iota(jnp.int32, (Q, Q), 1)

    def blk(j):
        return pl.ds(pl.multiple_of(j * Q, Q), Q)

    def split(t):
        hi = t.astype(BF16)
        return hi, (t - hi.astype(F32)).astype(BF16)

    @pl.when(i == 0)
    def _():
        for j in range(nq):
            hi, lo = split(kx_ref[j * Q:(j + 1) * Q, :])
            kx3_ref[j * Q:(j + 1) * Q, :] = (_mm(hi, perm_ref[0]) + _mm(lo, perm_ref[1])).astype(BF16)

    rhs = []
    for h in range(IDX_HEADS):
        hi, lo = split(qit_ref[h * IDX_DIM:(h + 1) * IDX_DIM, :])
        rhs.append(jnp.concatenate([hi, hi, lo, jnp.zeros_like(hi)], axis=0))

    def score_body(j, _):
        kt3 = kx3_ref[blk(j), :]
        acc = jnp.zeros((Q, Q), F32)
        for h in range(IDX_HEADS):
            acc = acc + jnp.maximum(_mm(kt3, rhs[h]), 0.0) * wt_ref[h:h + 1, :]
        acc = jnp.where(acc == 0.0, 0.0, acc)
        bits = pltpu.bitcast(acc, jnp.int32)
        key = jnp.where(bits < 0, bits ^ jnp.int32(0x7FFFFFFF), bits)
        causal = (j * Q + krow) <= (i * Q + qcol)
        keys_ref[blk(j), :] = jnp.where(causal, key, jnp.int32(INT_MIN))
        return 0

    lax.fori_loop(0, i + 1, score_body, 0)

    def count(pred):
        def body(j, acc8):
            hit = jnp.where(pred(keys_ref[blk(j), :], j * Q), 1.0, 0.0)
            parts = [hit[8 * r:8 * r + 8, :] for r in range(Q // 8)]
            while len(parts) > 1:
                parts = [x + y for x, y in zip(parts[::2], parts[1::2])]
            return acc8 + parts[0]
        acc8 = lax.fori_loop(0, i + 1, body, jnp.zeros((8, Q), F32))
        return jnp.sum(acc8, axis=0, keepdims=True)

    def thr_body(it, thr):
        cand = thr + jnp.left_shift(jnp.int32(1), 31 - it)
        return jnp.where(count(lambda kb, k0: kb >= cand) >= n_keep, cand, thr)

    thr = lax.fori_loop(0, 32, thr_body, jnp.full((1, Q), INT_MIN, jnp.int32))

    n_gt = count(lambda kb, k0: kb > thr)
    n_ge = count(lambda kb, k0: kb >= thr)
    need = n_keep - n_gt
    excess = jnp.where((n_ge - n_gt > need) & (thr > jnp.int32(INT_MIN)), 1.0, 0.0)
    cut_ref[...] = jnp.full(cut_ref.shape, 2 ** 30, jnp.int32)

    @pl.when(jnp.max(excess) > 0.0)
    def _():
        def cut_body(it, cut):
            cand = cut + jnp.left_shift(jnp.int32(1), n_idx_bits - 1 - it)
            cnt = count(lambda kb, k0: (kb == thr) & ((k0 + krow) < cand))
            return jnp.where(cnt <= need, cand, cut)
        cut_ref[0:1, :] = lax.fori_loop(0, n_idx_bits, cut_body, jnp.zeros((1, Q), jnp.int32))

    cut = cut_ref[0:1, :]

    def mask_body(j, _):
        kb = keys_ref[blk(j), :]
        keep = (kb > thr) | ((kb == thr) & ((j * Q + krow) < cut) & (kb > jnp.int32(INT_MIN)))
        mask_ref[blk(j), :] = jnp.where(keep, 0.0, NEG_INF)
        return 0

    lax.fori_loop(0, i + 1, mask_body, 0)

    def mask_fn(h, j, st, kind):
        st = st + mask_ref[blk(j), :]
        return jnp.where(i >= 1, st, NEG_INF) if kind == "prev" else st

    _masked_attention(i, qt_ref, k_ref, vt_ref, bias_ref, o_ref, mask_fn)


def _dsa_call(dqt, dk, dvt, diqt, dik, diwt, bias_t, perm, B, S):
    nq = S // ATT_BLOCK
    T = B * S
    n_keep = min(DSA_TOPK_MAX, S // 4)
    qcol = lambda c: pl.BlockSpec((c, ATT_BLOCK), lambda b, i: (0, b * nq + i))
    return pl.pallas_call(
        functools.partial(_dsa_kernel, n_keep=n_keep, n_idx_bits=int(math.log2(S)) + 1, nq=nq),
        grid=(B, nq),
        in_specs=[qcol(BRANCH_W),
                  pl.BlockSpec((S, BRANCH_W), lambda b, i: (b, 0)),
                  pl.BlockSpec((BRANCH_W, S), lambda b, i: (0, b)),
                  qcol(IDX_HEADS * IDX_DIM), qcol(IDX_HEADS),
                  pl.BlockSpec((S, 128), lambda b, i: (b, 0)),
                  _resident(bias_t.shape), _resident(perm.shape)],
        out_specs=qcol(BRANCH_W),
        out_shape=jax.ShapeDtypeStruct((BRANCH_W, T), BF16),
        scratch_shapes=[pltpu.VMEM((S, 128), BF16), pltpu.VMEM((S, ATT_BLOCK), jnp.int32),
                        pltpu.VMEM((S, ATT_BLOCK), F32), pltpu.VMEM((8, ATT_BLOCK), jnp.int32)],
        compiler_params=pltpu.CompilerParams(dimension_semantics=("arbitrary", "arbitrary"),
                                             vmem_limit_bytes=VMEM_LIMIT),
        name="dsa",
    )(dqt, dk, dvt, diqt, diwt, dik, bias_t, perm)


def _rwkv_pre_kernel(*refs, tiles_per_seq, has_vres):
    if has_vres:
        (rw_ref, prev_ref, mu_ref, vec_ref, w2_ref, a2_ref, g2_ref, gmat_ref,
         vf_ref, v0_ref, va_ref, vb_ref, r_o, lw_o, k_o, v_o, a_o, b_o, g_o) = refs
    else:
        (rw_ref, prev_ref, mu_ref, vec_ref, w2_ref, a2_ref, g2_ref, gmat_ref,
         r_o, lw_o, k_o, v_o, a_o, b_o, g_o) = refs
    i = pl.program_id(0)
    p = rw_ref[...]
    tm = p.shape[0]
    last = prev_ref[7:8, :]
    last = jnp.where(i % tiles_per_seq == 0, 0.0, last)
    rid = lax.broadcasted_iota(jnp.int32, p.shape, 0)
    shifted = jnp.where(rid == 0, last, pltpu.roll(p, 1, 0))
    p = p + (shifted - p) * mu_ref[...]
    r = p[:, 0:512]
    k = p[:, 512:1024]
    v = p[:, 1024:1536]
    wd = p[:, 1536:1536 + LORA_DECAY]
    ad = p[:, 1664:1664 + LORA_AAA]
    gd = p[:, 1792:2048]
    w0, a0, k_k, k_a = vec_ref[0:1, :], vec_ref[1:2, :], vec_ref[2:3, :], vec_ref[3:4, :]
    z = -(w0 + _mm(jnp.tanh(wd), w2_ref[...], HIGHEST))
    softplus = jnp.maximum(z, 0.0) + jnp.log(1.0 + jnp.exp(-jnp.abs(z)))
    w = -softplus - 0.5
    lw_o[...] = -jnp.exp(w)
    a = jax.nn.sigmoid(a0 + _mm(ad, a2_ref[...], HIGHEST))
    g_o[...] = _mm(jax.nn.sigmoid(gd), g2_ref[...], HIGHEST)
    if has_vres:
        mix = jax.nn.sigmoid(v0_ref[...] + _mm(_mm(v, va_ref[...], HIGHEST), vb_ref[...], HIGHEST))
        v = v + (vf_ref[...] - v) * mix
    kk = k * k_k
    nrm = jnp.sqrt(_group_sum(kk * kk, gmat_ref[...]))
    kk = kk / jnp.maximum(nrm, 1e-12)
    r_o[...] = r
    k_o[...] = k * (1.0 + (a - 1.0) * k_a)
    v_o[...] = v
    a_o[...] = -kk
    b_o[...] = kk * a


def _rwkv_pre_call(rw, mu, vec, w2, a2, g2, gmat, S, vres=None, tm=256):
    T = rw.shape[0]
    tiles_per_seq = S // tm
    row = lambda w: pl.BlockSpec((tm, w), lambda i: (i, 0))
    prev = pl.BlockSpec((8, RW_W), lambda i: (jnp.maximum(i * (tm // 8) - 1, 0), 0))
    ins = [rw, rw, mu, vec, w2, a2, g2, gmat]
    specs = [row(RW_W), prev] + [_resident(t.shape) for t in ins[2:]]
    if vres is not None:
        vf, v0, va, vb = vres
        ins += [vf, v0, va, vb]
        specs += [row(BRANCH_W), _resident(v0.shape), _resident(va.shape), _resident(vb.shape)]
    return pl.pallas_call(
        functools.partial(_rwkv_pre_kernel, tiles_per_seq=tiles_per_seq, has_vres=vres is not None),
        grid=(T // tm,),
        in_specs=specs,
        out_specs=[row(BRANCH_W)] * 7,
        out_shape=[jax.ShapeDtypeStruct((T, BRANCH_W), F32)] * 7,
        compiler_params=pltpu.CompilerParams(dimension_semantics=("arbitrary",),
                                             vmem_limit_bytes=VMEM_LIMIT),
        name="rwkv_pre",
    )(*ins)


def _rwkv_scan_kernel(r_ref, lw_ref, k_ref, v_ref, a_ref, b_ref, g_ref, vec_ref, gmat_ref, tri_ref,
                      o_ref, state_ref, oraw_ref, *, n_chunks):
    C = RWKV_CHUNK
    N = HEAD_DIM
    bf = lambda t: t.astype(BF16)

    @pl.when(pl.program_id(1) == 0)
    def _():
        state_ref[...] = jnp.zeros_like(state_ref)

    rr = lax.broadcasted_iota(jnp.int32, (C, C), 0)
    cc = lax.broadcasted_iota(jnp.int32, (C, C), 1)
    strict = rr > cc
    incl = rr >= cc
    eye = jnp.where(rr == cc, 1.0, 0.0)

    lw = lw_ref[...]
    cum = _mm(tri_ref[...], lw, HIGHEST)
    r, k, v, a, b = r_ref[...], k_ref[...], v_ref[...], a_ref[...], b_ref[...]
    e_neg = jnp.exp(-cum)
    at = a * jnp.exp(cum - lw)
    rt = r * jnp.exp(cum)
    bt = b * e_neg
    kt = k * e_neg
    bp, kp, p_end = [], [], []
    for c in range(n_chunks):
        rows = slice(c * C, (c + 1) * C)
        cum_last = cum[(c + 1) * C - 1:(c + 1) * C, :]
        e_end = jnp.exp(cum_last - cum[rows])
        bp.append(b[rows] * e_end)
        kp.append(k[rows] * e_end)
        p_end.append(jnp.exp(cum_last))

    pairs = [(c, h) for c in range(n_chunks) for h in range(N_HEADS)]
    sl = lambda t, c, h: t[c * C:(c + 1) * C, h * N:(h + 1) * N]
    hl = lambda t, h: t[:, h * N:(h + 1) * N]
    each = lambda f: {p: f(*p) for p in pairs}

    aall = each(lambda c, h: _nt(bf(jnp.concatenate([sl(at, c, h), sl(rt, c, h)], axis=0)),
                                 bf(jnp.concatenate([sl(bt, c, h), sl(kt, c, h)], axis=0))))
    vb = each(lambda c, h: bf(sl(v, c, h)))
    akv = each(lambda c, h: _mm(bf(jnp.where(strict, aall[c, h][:C, C:], 0.0)), vb[c, h]))
    o_loc = each(lambda c, h: _mm(bf(jnp.where(incl, aall[c, h][C:, C:], 0.0)), vb[c, h]))
    a_rb = each(lambda c, h: bf(jnp.where(incl, aall[c, h][C:, :C], 0.0)))
    x = each(lambda c, h: jnp.where(strict, aall[c, h][:C, :C], 0.0))
    tinv = each(lambda c, h: eye + x[c, h])
    for _ in range(int(math.log2(C)) - 1):
        x = each(lambda c, h: _mm(bf(x[c, h]), bf(x[c, h])))
        tinv = each(lambda c, h: tinv[c, h] + _mm(bf(tinv[c, h]), bf(x[c, h])))
    w_t = each(lambda c, h: _mm(bf(tinv[c, h]), bf(sl(at, c, h))))
    u_loc = each(lambda c, h: _mm(bf(tinv[c, h]), bf(akv[c, h])))

    heads = range(N_HEADS)
    state = [state_ref[h] for h in heads]
    for c in range(n_chunks):
        xs = [_nt(bf(jnp.concatenate([w_t[c, h], sl(rt, c, h)], axis=0)), bf(state[h])) for h in heads]
        u = [xs[h][:C] + u_loc[c, h] for h in heads]
        upd = [_tn(bf(jnp.concatenate([u[h], sl(v, c, h)], axis=0)),
                   bf(jnp.concatenate([hl(bp[c], h), hl(kp[c], h)], axis=0))) for h in heads]
        state = [state[h] * hl(p_end[c], h) + upd[h] for h in heads]
        o = [xs[h][C:] + _mm(a_rb[c, h], bf(u[h])) + o_loc[c, h] for h in heads]
        for h in heads:
            oraw_ref[c * C:(c + 1) * C, h * N:(h + 1) * N] = o[h]
    for h in heads:
        state_ref[h] = state[h]

    gmat = gmat_ref[...]
    o = oraw_ref[...]
    ln_w, ln_b, r_k = vec_ref[0:1, :], vec_ref[1:2, :], vec_ref[2:3, :]
    mean = _group_sum(o, gmat) * (1.0 / N)
    d = o - mean
    var = _group_sum(d * d, gmat) * (1.0 / N)
    o = d * lax.rsqrt(var + RWKV_GN_EPS) * ln_w + ln_b
    bonus = _group_sum(r * k * r_k, gmat)
    o_ref[...] = ((o + bonus * v) * g_ref[...]).astype(o_ref.dtype)


def _chunk_cumsum_matrix(tt):
    t = np.arange(tt)
    same_chunk = (t[:, None] // RWKV_CHUNK) == (t[None, :] // RWKV_CHUNK)
    return (same_chunk & (t[:, None] >= t[None, :])).astype(np.float32)


def _rwkv_scan_call(r, lw, k, v, a, b, g, vec, gmat, tri, B, S, tt=RWKV_TILE):
    T = B * S
    nt = S // tt
    row = pl.BlockSpec((tt, BRANCH_W), lambda bi, t: (bi * nt + t, 0))
    return pl.pallas_call(
        functools.partial(_rwkv_scan_kernel, n_chunks=tt // RWKV_CHUNK),
        grid=(B, nt),
        in_specs=[row] * 7 + [_resident(vec.shape), _resident(gmat.shape), _resident(tri.shape)],
        out_specs=row,
        out_shape=jax.ShapeDtypeStruct((T, BRANCH_W), BF16),
        scratch_shapes=[pltpu.VMEM((N_HEADS, HEAD_DIM, HEAD_DIM), F32),
                        pltpu.VMEM((tt, BRANCH_W), F32)],
        compiler_params=pltpu.CompilerParams(dimension_semantics=("arbitrary", "arbitrary"),
                                             vmem_limit_bytes=VMEM_LIMIT),
        name="rwkv_scan",
    )(r, lw, k, v, a, b, g, vec, gmat, tri)


def _merge_ffn_kernel(x_ref, gate_ref, oa_ref, ob_ref, oc_ref, wbr_ref, wo_ref, nf_ref,
                      wfi_ref, wfo_ref, o_ref, act_ref, *, ff_chunk):
    mix = None
    for n, br in enumerate((oa_ref, ob_ref, oc_ref)):
        y = _mm(br[...], wbr_ref[n]) if n == 1 else _tn(br[...], wbr_ref[n])
        gt = jax.nn.sigmoid(gate_ref[:, n * D_MODEL:(n + 1) * D_MODEL].astype(F32))
        mix = gt * y if mix is None else mix + gt * y
    x = x_ref[...] + _mm(mix.astype(BF16), wo_ref[...])
    ms = jnp.mean(x * x, axis=-1, keepdims=True)
    h = (x * lax.rsqrt(ms + NORM_EPS) * nf_ref[...]).astype(BF16)
    for c in range(0, D_FF, ff_chunk):
        fg = _mm(h, wfi_ref[:, c:c + ff_chunk])
        fu = _mm(h, wfi_ref[:, D_FF + c:D_FF + c + ff_chunk])
        act_ref[:, c:c + ff_chunk] = (fg * jax.nn.sigmoid(fg) * fu).astype(BF16)
    o_ref[...] = x + _mm(act_ref[...], wfo_ref[...])


def _merge_ffn_call(x2, gate, oa, ob, oc, wbr, wo, nf, wfi, wfo, tm=512, ff_chunk=256):
    T = x2.shape[0]
    row = lambda w: pl.BlockSpec((tm, w), lambda i: (i, 0))
    col = pl.BlockSpec((BRANCH_W, tm), lambda i: (0, i))
    return pl.pallas_call(
        functools.partial(_merge_ffn_kernel, ff_chunk=ff_chunk),
        grid=(T // tm,),
        in_specs=[row(D_MODEL), row(3 * D_MODEL), col, row(BRANCH_W), col,
                  _resident(wbr.shape), _resident(wo.shape), _resident(nf.shape),
                  _resident(wfi.shape), _resident(wfo.shape)],
        out_specs=row(D_MODEL),
        out_shape=jax.ShapeDtypeStruct((T, D_MODEL), F32),
        scratch_shapes=[pltpu.VMEM((tm, D_FF), BF16)],
        compiler_params=pltpu.CompilerParams(dimension_semantics=("arbitrary",),
                                             vmem_limit_bytes=VMEM_LIMIT),
        name="merge_ffn",
    )(x2, gate, oa, ob, oc, wbr, wo, nf, wfi, wfo)


def _bucket_table():
    n = np.arange(2 * ATT_BLOCK)
    max_exact = REL_BUCKETS // 2
    nf = np.maximum(n, 1).astype(np.float32)
    large = max_exact + (np.log(nf / max_exact) / math.log(REL_MAX_DIST / max_exact)
                         * (REL_BUCKETS - max_exact)).astype(np.int32)
    large = np.minimum(large, REL_BUCKETS - 1)
    return np.where(n < max_exact, n, large).astype(np.int32)


def _bias_strips(rel_bias):
    period = 3 * ATT_BLOCK
    u = np.arange(period)
    dist = ATT_BLOCK + np.where(u < ATT_BLOCK, u, u - period)
    vec = jnp.where(jnp.asarray(dist >= 0)[None, :], rel_bias[_bucket_table()[np.maximum(dist, 0)]].T, NEG_INF)
    heads = vec.shape[0]
    skew = jnp.tile(vec, (1, 2 * ATT_BLOCK))[:, :2 * ATT_BLOCK * (period - 1)]
    return skew.reshape(heads, 2 * ATT_BLOCK, period - 1)[:, :, :ATT_BLOCK]


def _pad_cols(t, width):
    return jnp.pad(t, ((0, 0), (0, width - t.shape[1])))


def _pad_rows(t, height):
    return jnp.pad(t, ((0, height - t.shape[0]), (0, 0)))


def _split_w_in(w):
    o = 0
    def take(n):
        nonlocal o
        s = w[:, o:o + n]
        o += n
        return s
    mq, mk, mv = take(512), take(512), take(512)
    r, wd, k, v, ad, gd = take(512), take(LORA_DECAY), take(512), take(512), take(LORA_AAA), take(LORA_GATE)
    dq, dc, diq, dik, diw = take(512), take(DSA_KV_RANK), take(256), take(IDX_DIM), take(IDX_HEADS)
    gate = take(3 * D_MODEL)
    w_tm = jnp.concatenate([gate, mk, r, k, v, _pad_cols(wd, 128), _pad_cols(ad, 128), _pad_cols(gd, 256),
                            dc, _pad_cols(dik, 128)], axis=1)
    w_cm = jnp.concatenate([mq, mv, dq, diq, diw], axis=1).T
    return w_tm.astype(BF16), w_cm.astype(BF16)


def _index_key_perm():
    p = np.zeros((2, 128, 128), np.float32)
    c = np.arange(IDX_DIM)
    p[0, c, c] = 1.0
    p[0, c, 2 * IDX_DIM + c] = 1.0
    p[1, c, IDX_DIM + c] = 1.0
    return jnp.asarray(p, BF16)


def _permute_mu(mu):
    r, wd, k, v, ad, gd = jnp.split(mu, np.cumsum([512, LORA_DECAY, 512, 512, LORA_AAA])[:].tolist())
    z = lambda n: jnp.zeros((n,), F32)
    return jnp.concatenate([r, k, v, wd, z(64), ad, z(64), gd, z(256 - LORA_GATE)])[None, :]


def kernel(x, w_in, norm_mix, norm_ffn, qk_norm, rel_bias, rwkv_mu, rwkv_w0, rwkv_w2, rwkv_a0, rwkv_a2, rwkv_g2, rwkv_kk, rwkv_ka, rwkv_rk, rwkv_ln_w, rwkv_ln_b, rwkv_v0, rwkv_va, rwkv_vb, dsa_kv_norm, dsa_kv_up, w_branch, w_o, w_ffn_in, w_ffn_out):
    B, S, D = x.shape
    depth = w_in.shape[0]
    T = B * S
    strips = _bias_strips(rel_bias * LOG2E)
    moba_bias, dsa_bias = strips[:N_HEADS], strips[N_HEADS:]
    head_of_lane = np.arange(BRANCH_W) // HEAD_DIM
    gmat = jnp.asarray(head_of_lane[:, None] == head_of_lane[None, :], BF16)
    tri = jnp.asarray(_chunk_cumsum_matrix(RWKV_TILE))
    perm = _index_key_perm()
    x2 = x.reshape(T, D)
    v_first = None
    for l in range(depth):
        w_tm, w_cm = _split_w_in(w_in[l])
        per_channel = lambda g: jnp.tile(g, N_HEADS)
        gq = jnp.stack([per_channel(qk_norm[l, 0]), per_channel(qk_norm[l, 2])]) * (LOG2E * HEAD_DIM ** -0.5)
        gq = jnp.broadcast_to(gq[:, :, None], (2, BRANCH_W, ATT_BLOCK))
        gk = jnp.stack([per_channel(qk_norm[l, 1]), per_channel(qk_norm[l, 3])])
        kv_up = dsa_kv_up[l].astype(BF16)
        (gate, mqt, mk, mvt, rw, dqt, dk, dvt, diqt, dik, diwt) = _proj_call(
            x2, norm_mix[l][None, :], w_tm, w_cm, dsa_kv_norm[l][None, :],
            kv_up[:, :BRANCH_W], kv_up[:, BRANCH_W:].T, gk, gq, gmat)
        o_moba = _moba_call(mqt, mk, mvt, moba_bias, B, S)
        o_dsa = _dsa_call(dqt, dk, dvt, diqt, dik, diwt, dsa_bias, perm, B, S)
        vec = jnp.stack([rwkv_w0[l], rwkv_a0[l], rwkv_kk[l], rwkv_ka[l]])
        vres = None
        if l > 0:
            vres = (v_first, rwkv_v0[l - 1][None, :], _pad_cols(rwkv_va[l - 1], 128),
                    _pad_rows(rwkv_vb[l - 1], 128))
        r, lw, k, v, a, b, g = _rwkv_pre_call(
            rw, _permute_mu(rwkv_mu[l]), vec, rwkv_w2[l], rwkv_a2[l], _pad_rows(rwkv_g2[l], 256),
            gmat, S, vres)
        if l == 0:
            v_first = v
        vec2 = jnp.stack([rwkv_ln_w[l], rwkv_ln_b[l], rwkv_rk[l].reshape(-1)])
        o_rwkv = _rwkv_scan_call(r, lw, k, v, a, b, g, vec2, gmat, tri, B, S)
        x2 = _merge_ffn_call(x2, gate, o_moba, o_rwkv, o_dsa, w_branch[l].astype(BF16),
                             w_o[l].astype(BF16), norm_ffn[l][None, :],
                             w_ffn_in[l].astype(BF16), w_ffn_out[l].astype(BF16))
    return x2.reshape(B, S, D)
```

```python
import functools
import math

import numpy as np
import jax
import jax.numpy as jnp
from jax import lax
from jax.experimental import pallas as pl
from jax.experimental.pallas import tpu as pltpu

F32 = jnp.float32
BF16 = jnp.bfloat16
HIGHEST = lax.Precision.HIGHEST

D_MODEL = 1024
HEAD_DIM = 64
BRANCH_W = 512
N_HEADS = BRANCH_W // HEAD_DIM
MOBA_BLOCK = 256
MOBA_TOPK = 3
LORA_DECAY = 64
LORA_AAA = 64
LORA_GATE = 160
LORA_MV = 32
RWKV_GN_EPS = 64e-5
DSA_KV_RANK = 256
IDX_HEADS = 8
IDX_DIM = 32
DSA_TOPK_MAX = 256
REL_BUCKETS = 32
REL_MAX_DIST = 128
D_FF = 2816
NORM_EPS = 1e-6

ATT_BLOCK = 256
RWKV_CHUNK = 64
RWKV_TILE = 256
INT_MIN = -(2 ** 31)
NEG_INF = float("-inf")
LOG2E = math.log2(math.e)
VMEM_LIMIT = 52 * 1024 * 1024

TM_GATE, TM_MK, TM_RW, TM_DC, TM_DIK, N_TM = 0, 3072, 3584, 5632, 5888, 6016
CM_MQ, CM_MV, CM_DQ, CM_DIQ, CM_DIW, N_CM = 0, 512, 1024, 1536, 1792, 1800
RW_W = 2048


def _nt(a, b, precision=None):
    return lax.dot_general(a, b, (((1,), (1,)), ((), ())), precision=precision,
                           preferred_element_type=F32)


def _tn(a, b, precision=None):
    return lax.dot_general(a, b, (((0,), (0,)), ((), ())), precision=precision,
                           preferred_element_type=F32)


def _mm(a, b, precision=None):
    return jnp.dot(a, b, precision=precision, preferred_element_type=F32)


def _group_sum(t, gmat):
    hi = t.astype(BF16)
    lo = (t - hi.astype(F32)).astype(BF16)
    return _mm(hi, gmat) + _mm(lo, gmat)


def _head_rms(t, gain, gmat):
    ms = _group_sum(t * t, gmat) * (1.0 / HEAD_DIM)
    return t * lax.rsqrt(ms + NORM_EPS) * gain


def _resident(shape):
    nd = len(shape)
    return pl.BlockSpec(shape, lambda *_: (0,) * nd, pipeline_mode=pl.Buffered(1))


def _head_rms_cm(t, gain):
    parts = []
    for h in range(N_HEADS):
        blk = t[h * HEAD_DIM:(h + 1) * HEAD_DIM, :]
        ms = jnp.mean(blk * blk, axis=0, keepdims=True)
        parts.append(blk * lax.rsqrt(ms + NORM_EPS))
    return jnp.concatenate(parts, axis=0) * gain


def _proj_kernel(x_ref, g_ref, wtm_ref, wcm_ref, kvn_ref, kvk_ref, kvvt_ref, qkn_ref, gq_ref, gmat_ref,
                 gate_o, mqt_o, mk_o, mvt_o, rw_o, dqt_o, dk_o, dvt_o, diqt_o, dik_o, diwt_o):
    x = x_ref[...]
    ms = jnp.mean(x * x, axis=-1, keepdims=True)
    h = (x * lax.rsqrt(ms + NORM_EPS) * g_ref[...]).astype(BF16)
    gmat = gmat_ref[...]

    def proj(off, width):
        return _mm(h, wtm_ref[:, off:off + width])

    def proj_t(off, width):
        return _nt(wcm_ref[off:off + width, :], h)

    gate_o[...] = proj(TM_GATE, 3 * D_MODEL).astype(BF16)
    mqt_o[...] = _head_rms_cm(proj_t(CM_MQ, BRANCH_W), gq_ref[0]).astype(BF16)
    mk_o[...] = _head_rms(proj(TM_MK, BRANCH_W), qkn_ref[0:1, :], gmat).astype(BF16)
    mvt_o[...] = proj_t(CM_MV, BRANCH_W).astype(BF16)
    rw_o[...] = proj(TM_RW, RW_W)
    dqt_o[...] = _head_rms_cm(proj_t(CM_DQ, BRANCH_W), gq_ref[1]).astype(BF16)
    dc = proj(TM_DC, DSA_KV_RANK)
    dcn = (dc * lax.rsqrt(jnp.mean(dc * dc, axis=-1, keepdims=True) + NORM_EPS) * kvn_ref[...]).astype(BF16)
    dk_o[...] = _head_rms(_mm(dcn, kvk_ref[...]), qkn_ref[1:2, :], gmat).astype(BF16)
    dvt_o[...] = _nt(kvvt_ref[...], dcn).astype(BF16)
    diqt_o[...] = proj_t(CM_DIQ, IDX_HEADS * IDX_DIM)
    dik_o[...] = proj(TM_DIK, 128)
    diwt_o[...] = proj_t(CM_DIW, IDX_HEADS)


def _proj_call(x2, g, w_tm, w_cm, kvn, kvk, kvvt, qkn, gq, gmat, tm=ATT_BLOCK):
    T = x2.shape[0]
    row = lambda w: pl.BlockSpec((tm, w), lambda i: (i, 0))
    col = lambda c: pl.BlockSpec((c, tm), lambda i: (0, i))
    tmaj = lambda w, d: (jax.ShapeDtypeStruct((T, w), d), row(w))
    cmaj = lambda c, d: (jax.ShapeDtypeStruct((c, T), d), col(c))
    outs = [tmaj(3 * D_MODEL, BF16), cmaj(BRANCH_W, BF16), tmaj(BRANCH_W, BF16), cmaj(BRANCH_W, BF16),
            tmaj(RW_W, F32), cmaj(BRANCH_W, BF16), tmaj(BRANCH_W, BF16), cmaj(BRANCH_W, BF16),
            cmaj(IDX_HEADS * IDX_DIM, F32), tmaj(128, F32), cmaj(IDX_HEADS, F32)]
    ins = [x2, g, w_tm, w_cm, kvn, kvk, kvvt, qkn, gq, gmat]
    return pl.pallas_call(
        _proj_kernel,
        grid=(T // tm,),
        in_specs=[row(D_MODEL)] + [_resident(t.shape) for t in ins[1:]],
        out_specs=[spec for _, spec in outs],
        out_shape=[shape for shape, _ in outs],
        compiler_params=pltpu.CompilerParams(dimension_semantics=("arbitrary",),
                                             vmem_limit_bytes=VMEM_LIMIT),
        name="proj",
    )(*ins)


ONES_ROWS = 16


def _masked_attention(i, qt_ref, k_ref, vt_ref, bias_ref, o_ref, mask_fn):
    Q = ATT_BLOCK
    H = range(N_HEADS)

    def blk(j):
        return pl.ds(pl.multiple_of(j * Q, Q), Q)

    prow = lax.broadcasted_iota(jnp.int32, (2 * HEAD_DIM, Q), 0)
    qm = []
    for p in range(N_HEADS // 2):
        pair = qt_ref[p * 2 * HEAD_DIM:(p + 1) * 2 * HEAD_DIM, :]
        zero = jnp.zeros_like(pair)
        qm += [jnp.where(prow < HEAD_DIM, pair, zero), jnp.where(prow >= HEAD_DIM, pair, zero)]
    ones = jnp.ones((ONES_ROWS, Q), BF16)

    def tile(j, carry, kind):
        kt = [k_ref[blk(j), p * 2 * HEAD_DIM:(p + 1) * 2 * HEAD_DIM] for p in range(N_HEADS // 2)]
        st = [_mm(kt[h // 2], qm[h]) for h in H]
        if kind == "prev":
            st = [st[h] + bias_ref[h, :Q, :] for h in H]
        elif kind == "own":
            st = [st[h] + bias_ref[h, Q:, :] for h in H]
        st = [mask_fn(h, j, st[h], kind) for h in H]
        tmax = [jnp.max(st[h], axis=0, keepdims=True) for h in H]
        if kind == "far":
            far = [bias_ref[h, 0:1, Q - 1:Q] for h in H]
            tmax = [tmax[h] + far[h] for h in H]
        m_new = [jnp.maximum(carry[h][0], tmax[h]) for h in H]
        alpha = [jnp.exp2(carry[h][0] - m_new[h]) for h in H]
        sub = [m_new[h] - far[h] for h in H] if kind == "far" else m_new
        p = [jnp.exp2(st[h] - sub[h]).astype(BF16) for h in H]
        vt = [jnp.concatenate([vt_ref[h * HEAD_DIM:(h + 1) * HEAD_DIM, blk(j)], ones], axis=0) for h in H]
        pv = [_mm(vt[h], p[h]) for h in H]
        return tuple((m_new[h], alpha[h] * carry[h][1] + pv[h]) for h in H)

    init = tuple((jnp.full((1, Q), -1e30, F32), jnp.zeros((HEAD_DIM + ONES_ROWS, Q), F32)) for _ in H)
    carry = lax.fori_loop(0, i - 1, lambda j, c: tile(j, c, "far"), init)
    carry = tile(jnp.maximum(i - 1, 0), carry, "prev")
    carry = tile(i, carry, "own")
    for h in H:
        acc = carry[h][1]
        o_ref[h * HEAD_DIM:(h + 1) * HEAD_DIM, :] = (acc[:HEAD_DIM] / acc[HEAD_DIM:HEAD_DIM + 1]).astype(o_ref.dtype)


def _moba_kernel(qt_ref, k_ref, vt_ref, bias_ref, o_ref, kmean_ref, sel_ref, *, nb):
    i = pl.program_id(1)
    Q = ATT_BLOCK

    @pl.when(i == 0)
    def _():
        means = [jnp.mean(k_ref[j * Q:(j + 1) * Q, :].astype(F32), axis=0, keepdims=True) for j in range(nb)]
        kmean = jnp.concatenate(means * N_HEADS, axis=0)
        shape = (N_HEADS * nb, BRANCH_W)
        row_head = lax.broadcasted_iota(jnp.int32, shape, 0) // nb
        lane_head = lax.broadcasted_iota(jnp.int32, shape, 1) // HEAD_DIM
        rest = jnp.where(row_head == lane_head, kmean, 0.0)
        for piece in range(3):
            part = rest.astype(BF16)
            kmean_ref[piece] = part
            rest = rest - part.astype(F32)

    qt = qt_ref[...]
    gate = _mm(kmean_ref[0], qt) + _mm(kmean_ref[1], qt) + _mm(kmean_ref[2], qt)
    gate = gate.reshape(N_HEADS, nb, Q)
    brow = lax.broadcasted_iota(jnp.int32, (N_HEADS, nb, Q), 1)
    n_sel = min(MOBA_TOPK, nb - 1)
    rank = jnp.zeros((N_HEADS, nb, Q), F32)
    for jp in range(nb):
        gj = gate[:, jp:jp + 1, :]
        ahead = (gj > gate) | ((gj == gate) & (jp < brow))
        rank = rank + jnp.where(ahead & (jp < i), 1.0, 0.0)
    sel_ref[...] = jnp.where((brow < i) & (rank < n_sel), 1.0, 0.0)

    def mask_fn(h, j, st, kind):
        if kind == "own":
            return st
        return jnp.where(sel_ref[h, pl.ds(j, 1), :] > 0.0, st, NEG_INF)

    _masked_attention(i, qt_ref, k_ref, vt_ref, bias_ref, o_ref, mask_fn)


def _moba_call(mqt, mk, mvt, bias_t, B, S):
    nb = S // ATT_BLOCK
    T = B * S
    qspec = pl.BlockSpec((BRANCH_W, ATT_BLOCK), lambda b, i: (0, b * nb + i))
    return pl.pallas_call(
        functools.partial(_moba_kernel, nb=nb),
        grid=(B, nb),
        in_specs=[qspec,
                  pl.BlockSpec((S, BRANCH_W), lambda b, i: (b, 0)),
                  pl.BlockSpec((BRANCH_W, S), lambda b, i: (0, b)),
                  _resident(bias_t.shape)],
        out_specs=qspec,
        out_shape=jax.ShapeDtypeStruct((BRANCH_W, T), BF16),
        scratch_shapes=[pltpu.VMEM((3, N_HEADS * nb, BRANCH_W), BF16),
                        pltpu.VMEM((N_HEADS, nb, ATT_BLOCK), F32)],
        compiler_params=pltpu.CompilerParams(dimension_semantics=("arbitrary", "arbitrary"),
                                             vmem_limit_bytes=VMEM_LIMIT),
        name="moba",
    )(mqt, mk, mvt, bias_t)


def _dsa_kernel(qt_ref, k_ref, vt_ref, qit_ref, wt_ref, kx_ref, bias_ref, perm_ref, o_ref,
                kx3_ref, keys_ref, khi_ref, klo_ref, mask_ref, cut_ref, *, n_keep, n_idx_bits, nq):
    i = pl.program_id(1)
    Q = ATT_BLOCK
    krow = lax.broadcasted_iota(jnp.int32, (Q, Q), 0)
    qcol = lax.broadcasted_iota(jnp.int32, (Q, Q), 1)

    def blk(j):
        return pl.ds(pl.multiple_of(j * Q, Q), Q)

    def split(t):
        hi = t.astype(BF16)
        return hi, (t - hi.astype(F32)).astype(BF16)

    @pl.when(i == 0)
    def _():
        for j in range(nq):
            hi, lo = split(kx_ref[j * Q:(j + 1) * Q, :])
            kx3_ref[j * Q:(j + 1) * Q, :] = (_mm(hi, perm_ref[0]) + _mm(lo, perm_ref[1])).astype(BF16)

    rhs = []
    for h in range(IDX_HEADS):
        hi, lo = split(qit_ref[h * IDX_DIM:(h + 1) * IDX_DIM, :])
        rhs.append(jnp.concatenate([hi, hi, lo, jnp.zeros_like(hi)], axis=0))

    def score_body(j, _):
        kt3 = kx3_ref[blk(j), :]
        acc = jnp.zeros((Q, Q), F32)
        for h in range(IDX_HEADS):
            acc = acc + jnp.maximum(_mm(kt3, rhs[h]), 0.0) * wt_ref[h:h + 1, :]
        acc = jnp.where(acc == 0.0, 0.0, acc)
        bits = pltpu.bitcast(acc, jnp.int32)
        key = jnp.where(bits < 0, bits ^ jnp.int32(0x7FFFFFFF), bits)
        causal = (j * Q + krow) <= (i * Q + qcol)
        key = jnp.where(causal, key, jnp.int32(INT_MIN))
        keys_ref[blk(j), :] = key
        khi_ref[blk(j), :] = (key >> 16).astype(jnp.int16)
        klo_ref[blk(j), :] = ((key & jnp.int32(0xFFFF)) - 32768).astype(jnp.int16)
        return 0

    lax.fori_loop(0, i + 1, score_body, 0)

    def count16(ref, pred):
        rows = 16
        def body(j, acc):
            hit = jnp.where(pred(ref[blk(j), :]), jnp.int16(1), jnp.int16(0))
            parts = [hit[rows * r:rows * (r + 1), :] for r in range(Q // rows)]
            while len(parts) > 1:
                parts = [x + y for x, y in zip(parts[::2], parts[1::2])]
            return acc + parts[0]
        acc = lax.fori_loop(0, i + 1, body, jnp.zeros((rows, Q), jnp.int16))
        return jnp.sum(acc.astype(jnp.int32), axis=0, keepdims=True)

    def bisect16(ref, target):
        def body(it, v):
            cand = v + jnp.left_shift(jnp.int32(1), 15 - it)
            c16 = cand.astype(jnp.int16)
            return jnp.where(count16(ref, lambda x: x >= c16) >= target, cand, v)
        return lax.fori_loop(0, 16, body, jnp.full((1, Q), -32768, jnp.int32))

    def count(pred):
        def body(j, acc8):
            hit = jnp.where(pred(keys_ref[blk(j), :], j * Q), 1.0, 0.0)
            parts = [hit[8 * r:8 * r + 8, :] for r in range(Q // 8)]
            while len(parts) > 1:
                parts = [x + y for x, y in zip(parts[::2], parts[1::2])]
            return acc8 + parts[0]
        acc8 = lax.fori_loop(0, i + 1, body, jnp.zeros((8, Q), F32))
        return jnp.sum(acc8, axis=0, keepdims=True)

    thr_hi = bisect16(khi_ref, n_keep)
    hi16 = thr_hi.astype(jnp.int16)
    need_lo = n_keep - count16(khi_ref, lambda x: x > hi16)

    def lo_mask_body(j, _):
        klo_ref[blk(j), :] = jnp.where(khi_ref[blk(j), :] == hi16, klo_ref[blk(j), :], jnp.int16(-32768))
        return 0

    lax.fori_loop(0, i + 1, lo_mask_body, 0)
    thr = thr_hi * 65536 + (bisect16(klo_ref, need_lo) + 32768)

    n_gt = count(lambda kb, k0: kb > thr)
    n_ge = count(lambda kb, k0: kb >= thr)
    need = n_keep - n_gt
    excess = jnp.where((n_ge - n_gt > need) & (thr > jnp.int32(INT_MIN)), 1.0, 0.0)
    cut_ref[...] = jnp.full(cut_ref.shape, 2 ** 30, jnp.int32)

    @pl.when(jnp.max(excess) > 0.0)
    def _():
        def cut_body(it, cut):
            cand = cut + jnp.left_shift(jnp.int32(1), n_idx_bits - 1 - it)
            cnt = count(lambda kb, k0: (kb == thr) & ((k0 + krow) < cand))
            return jnp.where(cnt <= need, cand, cut)
        cut_ref[0:1, :] = lax.fori_loop(0, n_idx_bits, cut_body, jnp.zeros((1, Q), jnp.int32))

    cut = cut_ref[0:1, :]

    def mask_body(j, _):
        kb = keys_ref[blk(j), :]
        keep = (kb > thr) | ((kb == thr) & ((j * Q + krow) < cut) & (kb > jnp.int32(INT_MIN)))
        mask_ref[blk(j), :] = jnp.where(keep, 0.0, NEG_INF)
        return 0

    lax.fori_loop(0, i + 1, mask_body, 0)

    def mask_fn(h, j, st, kind):
        st = st + mask_ref[blk(j), :]
        return jnp.where(i >= 1, st, NEG_INF) if kind == "prev" else st

    _masked_attention(i, qt_ref, k_ref, vt_ref, bias_ref, o_ref, mask_fn)


def _dsa_call(dqt, dk, dvt, diqt, dik, diwt, bias_t, perm, B, S):
    nq = S // ATT_BLOCK
    T = B * S
    n_keep = min(DSA_TOPK_MAX, S // 4)
    qcol = lambda c: pl.BlockSpec((c, ATT_BLOCK), lambda b, i: (0, b * nq + i))
    return pl.pallas_call(
        functools.partial(_dsa_kernel, n_keep=n_keep, n_idx_bits=int(math.log2(S)) + 1, nq=nq),
        grid=(B, nq),
        in_specs=[qcol(BRANCH_W),
                  pl.BlockSpec((S, BRANCH_W), lambda b, i: (b, 0)),
                  pl.BlockSpec((BRANCH_W, S), lambda b, i: (0, b)),
                  qcol(IDX_HEADS * IDX_DIM), qcol(IDX_HEADS),
                  pl.BlockSpec((S, 128), lambda b, i: (b, 0)),
                  _resident(bias_t.shape), _resident(perm.shape)],
        out_specs=qcol(BRANCH_W),
        out_shape=jax.ShapeDtypeStruct((BRANCH_W, T), BF16),
        scratch_shapes=[pltpu.VMEM((S, 128), BF16), pltpu.VMEM((S, ATT_BLOCK), jnp.int32),
                        pltpu.VMEM((S, ATT_BLOCK), jnp.int16), pltpu.VMEM((S, ATT_BLOCK), jnp.int16),
                        pltpu.VMEM((S, ATT_BLOCK), F32), pltpu.VMEM((8, ATT_BLOCK), jnp.int32)],
        compiler_params=pltpu.CompilerParams(dimension_semantics=("arbitrary", "arbitrary"),
                                             vmem_limit_bytes=VMEM_LIMIT),
        name="dsa",
    )(dqt, dk, dvt, diqt, diwt, dik, bias_t, perm)


def _rwkv_pre_kernel(*refs, tiles_per_seq, has_vres):
    if has_vres:
        (rw_ref, prev_ref, mu_ref, vec_ref, w2_ref, a2_ref, g2_ref, gmat_ref,
         vf_ref, v0_ref, va_ref, vb_ref, r_o, lw_o, k_o, v_o, a_o, b_o, g_o) = refs
    else:
        (rw_ref, prev_ref, mu_ref, vec_ref, w2_ref, a2_ref, g2_ref, gmat_ref,
         r_o, lw_o, k_o, v_o, a_o, b_o, g_o) = refs
    i = pl.program_id(0)
    p = rw_ref[...]
    tm = p.shape[0]
    last = prev_ref[7:8, :]
    last = jnp.where(i % tiles_per_seq == 0, 0.0, last)
    rid = lax.broadcasted_iota(jnp.int32, p.shape, 0)
    shifted = jnp.where(rid == 0, last, pltpu.roll(p, 1, 0))
    p = p + (shifted - p) * mu_ref[...]
    r = p[:, 0:512]
    k = p[:, 512:1024]
    v = p[:, 1024:1536]
    wd = p[:, 1536:1536 + LORA_DECAY]
    ad = p[:, 1664:1664 + LORA_AAA]
    gd = p[:, 1792:2048]
    w0, a0, k_k, k_a = vec_ref[0:1, :], vec_ref[1:2, :], vec_ref[2:3, :], vec_ref[3:4, :]
    z = -(w0 + _mm(jnp.tanh(wd), w2_ref[...], HIGHEST))
    softplus = jnp.maximum(z, 0.0) + jnp.log(1.0 + jnp.exp(-jnp.abs(z)))
    w = -softplus - 0.5
    lw_o[...] = -jnp.exp(w)
    a = jax.nn.sigmoid(a0 + _mm(ad, a2_ref[...], HIGHEST))
    g_o[...] = _mm(jax.nn.sigmoid(gd).astype(BF16), g2_ref[...])
    if has_vres:
        low = _mm(v.astype(BF16), va_ref[...])
        mix = jax.nn.sigmoid(v0_ref[...] + _mm(low.astype(BF16), vb_ref[...]))
        v = v + (vf_ref[...] - v) * mix
    kk = k * k_k
    nrm = jnp.sqrt(_group_sum(kk * kk, gmat_ref[...]))
    kk = kk / jnp.maximum(nrm, 1e-12)
    r_o[...] = r
    k_o[...] = k * (1.0 + (a - 1.0) * k_a)
    v_o[...] = v
    a_o[...] = -kk
    b_o[...] = kk * a


def _rwkv_pre_call(rw, mu, vec, w2, a2, g2, gmat, S, vres=None, tm=256):
    T = rw.shape[0]
    tiles_per_seq = S // tm
    row = lambda w: pl.BlockSpec((tm, w), lambda i: (i, 0))
    prev = pl.BlockSpec((8, RW_W), lambda i: (jnp.maximum(i * (tm // 8) - 1, 0), 0))
    ins = [rw, rw, mu, vec, w2, a2, g2, gmat]
    specs = [row(RW_W), prev] + [_resident(t.shape) for t in ins[2:]]
    if vres is not None:
        vf, v0, va, vb = vres
        ins += [vf, v0, va, vb]
        specs += [row(BRANCH_W), _resident(v0.shape), _resident(va.shape), _resident(vb.shape)]
    return pl.pallas_call(
        functools.partial(_rwkv_pre_kernel, tiles_per_seq=tiles_per_seq, has_vres=vres is not None),
        grid=(T // tm,),
        in_specs=specs,
        out_specs=[row(BRANCH_W)] * 7,
        out_shape=[jax.ShapeDtypeStruct((T, BRANCH_W), F32)] * 7,
        compiler_params=pltpu.CompilerParams(dimension_semantics=("arbitrary",),
                                             vmem_limit_bytes=VMEM_LIMIT),
        name="rwkv_pre",
    )(*ins)


def _rwkv_scan_kernel(r_ref, lw_ref, k_ref, v_ref, a_ref, b_ref, g_ref, vec_ref, gmat_ref, tri_ref,
                      o_ref, state_ref, oraw_ref, *, n_chunks):
    C = RWKV_CHUNK
    N = HEAD_DIM
    bf = lambda t: t.astype(BF16)

    @pl.when(pl.program_id(1) == 0)
    def _():
        state_ref[...] = jnp.zeros_like(state_ref)

    lw = lw_ref[...]
    cum = _mm(tri_ref[...], lw, HIGHEST)
    r, k, v, a, b = r_ref[...], k_ref[...], v_ref[...], a_ref[...], b_ref[...]
    e_neg = jnp.exp(-cum)
    at = a * jnp.exp(cum - lw)
    rt = r * jnp.exp(cum)
    bt = b * e_neg
    kt = k * e_neg
    bp, kp, p_end = [], [], []
    for c in range(n_chunks):
        rows = slice(c * C, (c + 1) * C)
        cum_last = cum[(c + 1) * C - 1:(c + 1) * C, :]
        e_end = jnp.exp(cum_last - cum[rows])
        bp.append(b[rows] * e_end)
        kp.append(k[rows] * e_end)
        p_end.append(jnp.exp(cum_last))

    GW = 4 * N
    groups = BRANCH_W // GW
    pairs = [(c, g) for c in range(n_chunks) for g in range(groups)]
    sl = lambda t, c, g: t[c * C:(c + 1) * C, g * GW:(g + 1) * GW]
    gl = lambda t, g: t[:, g * GW:(g + 1) * GW]
    each = lambda f: {p: f(*p) for p in pairs}
    same_head = (lax.broadcasted_iota(jnp.int32, (GW, GW), 0) // N) == (lax.broadcasted_iota(jnp.int32, (GW, GW), 1) // N)
    zero16 = jnp.zeros((GW, GW), BF16)

    def bdiag(y):
        yb = bf(y)
        return jnp.where(same_head, jnp.concatenate([yb] * (GW // C), axis=0), zero16)

    tcol = lax.broadcasted_iota(jnp.int32, (C, GW), 1) % C
    trow = lax.broadcasted_iota(jnp.int32, (C, GW), 0)
    strict, incl = trow > tcol, trow >= tcol
    eye = jnp.where(trow == tcol, 1.0, 0.0)
    stack = lambda x, y: bf(jnp.concatenate([x, y], axis=0))

    lhs = each(lambda c, g: stack(sl(at, c, g), sl(rt, c, g)))
    xb = each(lambda c, g: _nt(lhs[c, g], bdiag(sl(bt, c, g))))
    xk = each(lambda c, g: _nt(lhs[c, g], bdiag(sl(kt, c, g))))
    a_rb = each(lambda c, g: bf(jnp.where(incl, xb[c, g][C:], 0.0)))
    av = each(lambda c, g: _mm(stack(jnp.where(strict, xk[c, g][:C], 0.0), jnp.where(incl, xk[c, g][C:], 0.0)),
                               bdiag(sl(v, c, g))))
    x = each(lambda c, g: jnp.where(strict, xb[c, g][:C], 0.0))
    tinv = each(lambda c, g: eye + x[c, g])
    for _ in range(int(math.log2(C)) - 1):
        x = each(lambda c, g: _mm(bf(x[c, g]), bdiag(x[c, g])))
        tinv = each(lambda c, g: tinv[c, g] + _mm(bf(tinv[c, g]), bdiag(x[c, g])))
    w_t = each(lambda c, g: _mm(bf(tinv[c, g]), bdiag(sl(at, c, g))))
    u_loc = each(lambda c, g: _mm(bf(tinv[c, g]), bdiag(av[c, g][:C])))

    G = range(groups)
    state = [state_ref[g] for g in G]
    for c in range(n_chunks):
        xs = [_nt(stack(w_t[c, g], sl(rt, c, g)), bf(state[g])) for g in G]
        u = [xs[g][:C] + u_loc[c, g] for g in G]
        upd = [_tn(stack(u[g], sl(v, c, g)), stack(gl(bp[c], g), gl(kp[c], g))) for g in G]
        state = [jnp.where(same_head, state[g] * gl(p_end[c], g) + upd[g], 0.0) for g in G]
        o = [xs[g][C:] + _mm(a_rb[c, g], bdiag(u[g])) + av[c, g][C:] for g in G]
        for g in G:
            oraw_ref[c * C:(c + 1) * C, g * GW:(g + 1) * GW] = o[g]
    for g in G:
        state_ref[g] = state[g]

    gmat = gmat_ref[...]
    o = oraw_ref[...]
    ln_w, ln_b, r_k = vec_ref[0:1, :], vec_ref[1:2, :], vec_ref[2:3, :]
    mean = _group_sum(o, gmat) * (1.0 / N)
    d = o - mean
    var = _group_sum(d * d, gmat) * (1.0 / N)
    o = d * lax.rsqrt(var + RWKV_GN_EPS) * ln_w + ln_b
    bonus = _group_sum(r * k * r_k, gmat)
    o_ref[...] = ((o + bonus * v) * g_ref[...]).astype(o_ref.dtype)


def _chunk_cumsum_matrix(tt):
    t = np.arange(tt)
    same_chunk = (t[:, None] // RWKV_CHUNK) == (t[None, :] // RWKV_CHUNK)
    return (same_chunk & (t[:, None] >= t[None, :])).astype(np.float32)


def _rwkv_scan_call(r, lw, k, v, a, b, g, vec, gmat, tri, B, S, tt=RWKV_TILE):
    T = B * S
    nt = S // tt
    row = pl.BlockSpec((tt, BRANCH_W), lambda bi, t: (bi * nt + t, 0))
    return pl.pallas_call(
        functools.partial(_rwkv_scan_kernel, n_chunks=tt // RWKV_CHUNK),
        grid=(B, nt),
        in_specs=[row] * 7 + [_resident(vec.shape), _resident(gmat.shape), _resident(tri.shape)],
        out_specs=row,
        out_shape=jax.ShapeDtypeStruct((T, BRANCH_W), BF16),
        scratch_shapes=[pltpu.VMEM((N_HEADS // 4, 4 * HEAD_DIM, 4 * HEAD_DIM), F32),
                        pltpu.VMEM((tt, BRANCH_W), F32)],
        compiler_params=pltpu.CompilerParams(dimension_semantics=("arbitrary", "arbitrary"),
                                             vmem_limit_bytes=VMEM_LIMIT),
        name="rwkv_scan",
    )(r, lw, k, v, a, b, g, vec, gmat, tri)


def _merge_ffn_kernel(x_ref, gate_ref, oa_ref, ob_ref, oc_ref, wbr_ref, wo_ref, nf_ref,
                      wfi_ref, wfo_ref, o_ref, act_ref, *, ff_chunk):
    mix = None
    for n, br in enumerate((oa_ref, ob_ref, oc_ref)):
        y = _mm(br[...], wbr_ref[n]) if n == 1 else _tn(br[...], wbr_ref[n])
        gt = jax.nn.sigmoid(gate_ref[:, n * D_MODEL:(n + 1) * D_MODEL].astype(F32))
        mix = gt * y if mix is None else mix + gt * y
    x = x_ref[...] + _mm(mix.astype(BF16), wo_ref[...])
    ms = jnp.mean(x * x, axis=-1, keepdims=True)
    h = (x * lax.rsqrt(ms + NORM_EPS) * nf_ref[...]).astype(BF16)
    for c in range(0, D_FF, ff_chunk):
        fg = _mm(h, wfi_ref[:, c:c + ff_chunk])
        fu = _mm(h, wfi_ref[:, D_FF + c:D_FF + c + ff_chunk])
        act_ref[:, c:c + ff_chunk] = (fg * jax.nn.sigmoid(fg) * fu).astype(BF16)
    o_ref[...] = x + _mm(act_ref[...], wfo_ref[...])


def _merge_ffn_call(x2, gate, oa, ob, oc, wbr, wo, nf, wfi, wfo, tm=512, ff_chunk=256):
    T = x2.shape[0]
    row = lambda w: pl.BlockSpec((tm, w), lambda i: (i, 0))
    col = pl.BlockSpec((BRANCH_W, tm), lambda i: (0, i))
    return pl.pallas_call(
        functools.partial(_merge_ffn_kernel, ff_chunk=ff_chunk),
        grid=(T // tm,),
        in_specs=[row(D_MODEL), row(3 * D_MODEL), col, row(BRANCH_W), col,
                  _resident(wbr.shape), _resident(wo.shape), _resident(nf.shape),
                  _resident(wfi.shape), _resident(wfo.shape)],
        out_specs=row(D_MODEL),
        out_shape=jax.ShapeDtypeStruct((T, D_MODEL), F32),
        scratch_shapes=[pltpu.VMEM((tm, D_FF), BF16)],
        compiler_params=pltpu.CompilerParams(dimension_semantics=("arbitrary",),
                                             vmem_limit_bytes=VMEM_LIMIT),
        name="merge_ffn",
    )(x2, gate, oa, ob, oc, wbr, wo, nf, wfi, wfo)


def _bucket_table():
    n = np.arange(2 * ATT_BLOCK)
    max_exact = REL_BUCKETS // 2
    nf = np.maximum(n, 1).astype(np.float32)
    large = max_exact + (np.log(nf / max_exact) / math.log(REL_MAX_DIST / max_exact)
                         * (REL_BUCKETS - max_exact)).astype(np.int32)
    large = np.minimum(large, REL_BUCKETS - 1)
    return np.where(n < max_exact, n, large).astype(np.int32)


def _bias_strips(rel_bias):
    period = 3 * ATT_BLOCK
    u = np.arange(period)
    dist = ATT_BLOCK + np.where(u < ATT_BLOCK, u, u - period)
    vec = jnp.where(jnp.asarray(dist >= 0)[None, :], rel_bias[_bucket_table()[np.maximum(dist, 0)]].T, NEG_INF)
    heads = vec.shape[0]
    skew = jnp.tile(vec, (1, 2 * ATT_BLOCK))[:, :2 * ATT_BLOCK * (period - 1)]
    return skew.reshape(heads, 2 * ATT_BLOCK, period - 1)[:, :, :ATT_BLOCK]


def _pad_cols(t, width):
    return jnp.pad(t, ((0, 0), (0, width - t.shape[1])))


def _pad_rows(t, height):
    return jnp.pad(t, ((0, height - t.shape[0]), (0, 0)))


def _split_w_in(w):
    o = 0
    def take(n):
        nonlocal o
        s = w[:, o:o + n]
        o += n
        return s
    mq, mk, mv = take(512), take(512), take(512)
    r, wd, k, v, ad, gd = take(512), take(LORA_DECAY), take(512), take(512), take(LORA_AAA), take(LORA_GATE)
    dq, dc, diq, dik, diw = take(512), take(DSA_KV_RANK), take(256), take(IDX_DIM), take(IDX_HEADS)
    gate = take(3 * D_MODEL)
    w_tm = jnp.concatenate([gate, mk, r, k, v, _pad_cols(wd, 128), _pad_cols(ad, 128), _pad_cols(gd, 256),
                            dc, _pad_cols(dik, 128)], axis=1)
    w_cm = jnp.concatenate([mq, mv, dq, diq, diw], axis=1).T
    return w_tm.astype(BF16), w_cm.astype(BF16)


def _index_key_perm():
    p = np.zeros((2, 128, 128), np.float32)
    c = np.arange(IDX_DIM)
    p[0, c, c] = 1.0
    p[0, c, 2 * IDX_DIM + c] = 1.0
    p[1, c, IDX_DIM + c] = 1.0
    return jnp.asarray(p, BF16)


def _permute_mu(mu):
    r, wd, k, v, ad, gd = jnp.split(mu, np.cumsum([512, LORA_DECAY, 512, 512, LORA_AAA])[:].tolist())
    z = lambda n: jnp.zeros((n,), F32)
    return jnp.concatenate([r, k, v, wd, z(64), ad, z(64), gd, z(256 - LORA_GATE)])[None, :]


def kernel(x, w_in, norm_mix, norm_ffn, qk_norm, rel_bias, rwkv_mu, rwkv_w0, rwkv_w2, rwkv_a0, rwkv_a2, rwkv_g2, rwkv_kk, rwkv_ka, rwkv_rk, rwkv_ln_w, rwkv_ln_b, rwkv_v0, rwkv_va, rwkv_vb, dsa_kv_norm, dsa_kv_up, w_branch, w_o, w_ffn_in, w_ffn_out):
    B, S, D = x.shape
    depth = w_in.shape[0]
    T = B * S
    strips = _bias_strips(rel_bias * LOG2E)
    moba_bias, dsa_bias = strips[:N_HEADS], strips[N_HEADS:]
    head_of_lane = np.arange(BRANCH_W) // HEAD_DIM
    gmat = jnp.asarray(head_of_lane[:, None] == head_of_lane[None, :], BF16)
    tri = jnp.asarray(_chunk_cumsum_matrix(RWKV_TILE))
    perm = _index_key_perm()
    x2 = x.reshape(T, D)
    v_first = None
    for l in range(depth):
        w_tm, w_cm = _split_w_in(w_in[l])
        per_channel = lambda g: jnp.tile(g, N_HEADS)
        gq = jnp.stack([per_channel(qk_norm[l, 0]), per_channel(qk_norm[l, 2])]) * (LOG2E * HEAD_DIM ** -0.5)
        gq = jnp.broadcast_to(gq[:, :, None], (2, BRANCH_W, ATT_BLOCK))
        gk = jnp.stack([per_channel(qk_norm[l, 1]), per_channel(qk_norm[l, 3])])
        kv_up = dsa_kv_up[l].astype(BF16)
        (gate, mqt, mk, mvt, rw, dqt, dk, dvt, diqt, dik, diwt) = _proj_call(
            x2, norm_mix[l][None, :], w_tm, w_cm, dsa_kv_norm[l][None, :],
            kv_up[:, :BRANCH_W], kv_up[:, BRANCH_W:].T, gk, gq, gmat)
        o_moba = _moba_call(mqt, mk, mvt, moba_bias, B, S)
        o_dsa = _dsa_call(dqt, dk, dvt, diqt, dik, diwt, dsa_bias, perm, B, S)
        vec = jnp.stack([rwkv_w0[l], rwkv_a0[l], rwkv_kk[l], rwkv_ka[l]])
        vres = None
        if l > 0:
            vres = (v_first, rwkv_v0[l - 1][None, :], _pad_cols(rwkv_va[l - 1], 128).astype(BF16),
                    _pad_rows(rwkv_vb[l - 1], 128).astype(BF16))
        r, lw, k, v, a, b, g = _rwkv_pre_call(
            rw, _permute_mu(rwkv_mu[l]), vec, rwkv_w2[l], rwkv_a2[l], _pad_rows(rwkv_g2[l], 256).astype(BF16),
            gmat, S, vres)
        if l == 0:
            v_first = v
        vec2 = jnp.stack([rwkv_ln_w[l], rwkv_ln_b[l], rwkv_rk[l].reshape(-1)])
        o_rwkv = _rwkv_scan_call(r, lw, k, v, a, b, g, vec2, gmat, tri, B, S)
        x2 = _merge_ffn_call(x2, gate, o_moba, o_rwkv, o_dsa, w_branch[l].astype(BF16),
                             w_o[l].astype(BF16), norm_ffn[l][None, :],
                             w_ffn_in[l].astype(BF16), w_ffn_out[l].astype(BF16))
    return x2.reshape(B, S, D)
```

```python
import functools
import math

import numpy as np
import jax
import jax.numpy as jnp
from jax import lax
from jax.experimental import pallas as pl
from jax.experimental.pallas import tpu as pltpu

F32 = jnp.float32
BF16 = jnp.bfloat16
HIGHEST = lax.Precision.HIGHEST

D_MODEL = 1024
HEAD_DIM = 64
BRANCH_W = 512
N_HEADS = BRANCH_W // HEAD_DIM
MOBA_BLOCK = 256
MOBA_TOPK = 3
LORA_DECAY = 64
LORA_AAA = 64
LORA_GATE = 160
LORA_MV = 32
RWKV_GN_EPS = 64e-5
DSA_KV_RANK = 256
IDX_HEADS = 8
IDX_DIM = 32
DSA_TOPK_MAX = 256
REL_BUCKETS = 32
REL_MAX_DIST = 128
D_FF = 2816
NORM_EPS = 1e-6

ATT_BLOCK = 256
RWKV_CHUNK = 64
RWKV_TILE = 256
INT_MIN = -(2 ** 31)
NEG_INF = float("-inf")
LOG2E = math.log2(math.e)
VMEM_LIMIT = 52 * 1024 * 1024

TM_GATE, TM_MK, TM_RW, TM_DC, TM_DIK, N_TM = 0, 3072, 3584, 5632, 5888, 6016
CM_MQ, CM_MV, CM_DQ, CM_DIQ, CM_DIW, N_CM = 0, 512, 1024, 1536, 1792, 1800
RW_W = 2048


def _nt(a, b, precision=None):
    return lax.dot_general(a, b, (((1,), (1,)), ((), ())), precision=precision,
                           preferred_element_type=F32)


def _tn(a, b, precision=None):
    return lax.dot_general(a, b, (((0,), (0,)), ((), ())), precision=precision,
                           preferred_element_type=F32)


def _mm(a, b, precision=None):
    return jnp.dot(a, b, precision=precision, preferred_element_type=F32)


def _group_sum(t, gmat):
    hi = t.astype(BF16)
    lo = (t - hi.astype(F32)).astype(BF16)
    return _mm(hi, gmat) + _mm(lo, gmat)


def _head_rms(t, gain, gmat):
    ms = _mm((t * t).astype(BF16), gmat) * (1.0 / HEAD_DIM)
    return t * lax.rsqrt(ms + NORM_EPS) * gain


def _resident(shape):
    nd = len(shape)
    return pl.BlockSpec(shape, lambda *_: (0,) * nd, pipeline_mode=pl.Buffered(1))


def _head_rms_cm(t, gain):
    parts = []
    for h in range(N_HEADS):
        blk = t[h * HEAD_DIM:(h + 1) * HEAD_DIM, :]
        ms = jnp.mean(blk * blk, axis=0, keepdims=True)
        parts.append(blk * lax.rsqrt(ms + NORM_EPS))
    return jnp.concatenate(parts, axis=0) * gain


def _proj_kernel(x_ref, g_ref, wtm_ref, wcm_ref, kvn_ref, kvk_ref, kvvt_ref, qkn_ref, gq_ref, gmat_ref,
                 gate_o, mqt_o, mk_o, mvt_o, rw_o, dqt_o, dk_o, dvt_o, diqt_o, dik_o, diwt_o):
    x = x_ref[...]
    ms = jnp.mean(x * x, axis=-1, keepdims=True)
    h = (x * lax.rsqrt(ms + NORM_EPS) * g_ref[...]).astype(BF16)
    gmat = gmat_ref[...]

    def proj(off, width):
        return _mm(h, wtm_ref[:, off:off + width])

    def proj_t(off, width):
        return _nt(wcm_ref[off:off + width, :], h)

    gate_o[...] = proj(TM_GATE, 3 * D_MODEL).astype(BF16)
    mqt_o[...] = _head_rms_cm(proj_t(CM_MQ, BRANCH_W), gq_ref[0]).astype(BF16)
    mk_o[...] = _head_rms(proj(TM_MK, BRANCH_W), qkn_ref[0:1, :], gmat).astype(BF16)
    mvt_o[...] = proj_t(CM_MV, BRANCH_W).astype(BF16)
    rw_o[...] = proj(TM_RW, RW_W)
    dqt_o[...] = _head_rms_cm(proj_t(CM_DQ, BRANCH_W), gq_ref[1]).astype(BF16)
    dc = proj(TM_DC, DSA_KV_RANK)
    dcn = (dc * lax.rsqrt(jnp.mean(dc * dc, axis=-1, keepdims=True) + NORM_EPS) * kvn_ref[...]).astype(BF16)
    dk_o[...] = _head_rms(_mm(dcn, kvk_ref[...]), qkn_ref[1:2, :], gmat).astype(BF16)
    dvt_o[...] = _nt(kvvt_ref[...], dcn).astype(BF16)
    diqt_o[...] = proj_t(CM_DIQ, IDX_HEADS * IDX_DIM)
    dik_o[...] = proj(TM_DIK, 128)
    diwt_o[...] = proj_t(CM_DIW, IDX_HEADS)


def _proj_call(x2, g, w_tm, w_cm, kvn, kvk, kvvt, qkn, gq, gmat, tm=ATT_BLOCK):
    T = x2.shape[0]
    row = lambda w: pl.BlockSpec((tm, w), lambda i: (i, 0))
    col = lambda c: pl.BlockSpec((c, tm), lambda i: (0, i))
    tmaj = lambda w, d: (jax.ShapeDtypeStruct((T, w), d), row(w))
    cmaj = lambda c, d: (jax.ShapeDtypeStruct((c, T), d), col(c))
    outs = [tmaj(3 * D_MODEL, BF16), cmaj(BRANCH_W, BF16), tmaj(BRANCH_W, BF16), cmaj(BRANCH_W, BF16),
            tmaj(RW_W, F32), cmaj(BRANCH_W, BF16), tmaj(BRANCH_W, BF16), cmaj(BRANCH_W, BF16),
            cmaj(IDX_HEADS * IDX_DIM, F32), tmaj(128, F32), cmaj(IDX_HEADS, F32)]
    ins = [x2, g, w_tm, w_cm, kvn, kvk, kvvt, qkn, gq, gmat]
    return pl.pallas_call(
        _proj_kernel,
        grid=(T // tm,),
        in_specs=[row(D_MODEL)] + [_resident(t.shape) for t in ins[1:]],
        out_specs=[spec for _, spec in outs],
        out_shape=[shape for shape, _ in outs],
        compiler_params=pltpu.CompilerParams(dimension_semantics=("arbitrary",),
                                             vmem_limit_bytes=VMEM_LIMIT),
        name="proj",
    )(*ins)


ONES_ROWS = 16


def _masked_attention(i, qt_ref, k_ref, vt_ref, bias_ref, o_ref, mask_fn=None, query_mask_fn=None):
    Q = ATT_BLOCK
    H = range(N_HEADS)

    def blk(j):
        return pl.ds(pl.multiple_of(j * Q, Q), Q)

    prow = lax.broadcasted_iota(jnp.int32, (2 * HEAD_DIM, Q), 0)
    qm = []
    for p in range(N_HEADS // 2):
        pair = qt_ref[p * 2 * HEAD_DIM:(p + 1) * 2 * HEAD_DIM, :]
        zero = jnp.zeros_like(pair)
        qm += [jnp.where(prow < HEAD_DIM, pair, zero), jnp.where(prow >= HEAD_DIM, pair, zero)]
    ones = jnp.ones((ONES_ROWS, Q), BF16)

    def tile(j, carry, kind):
        kt = [k_ref[blk(j), p * 2 * HEAD_DIM:(p + 1) * 2 * HEAD_DIM] for p in range(N_HEADS // 2)]
        st = [_mm(kt[h // 2], qm[h]) for h in H]
        if kind == "prev":
            st = [st[h] + bias_ref[h, :Q, :] for h in H]
        elif kind == "own":
            st = [st[h] + bias_ref[h, Q:, :] for h in H]
        if mask_fn is not None:
            st = [mask_fn(h, j, st[h], kind) for h in H]
        tmax = [jnp.max(st[h], axis=0, keepdims=True) for h in H]
        attends = [None if query_mask_fn is None else query_mask_fn(h, j, kind) for h in H]
        tmax = [tmax[h] if attends[h] is None else jnp.where(attends[h], tmax[h], NEG_INF) for h in H]
        if kind == "far":
            far = [bias_ref[h, 0:1, Q - 1:Q] for h in H]
            tmax = [tmax[h] + far[h] for h in H]
        m_new = [jnp.maximum(carry[h][0], tmax[h]) for h in H]
        alpha = [jnp.exp2(carry[h][0] - m_new[h]) for h in H]
        sub = [m_new[h] - far[h] for h in H] if kind == "far" else m_new
        sub = [sub[h] if attends[h] is None else jnp.where(attends[h], sub[h], -NEG_INF) for h in H]
        p = [jnp.exp2(st[h] - sub[h]).astype(BF16) for h in H]
        vt = [jnp.concatenate([vt_ref[h * HEAD_DIM:(h + 1) * HEAD_DIM, blk(j)], ones], axis=0) for h in H]
        pv = [_mm(vt[h], p[h]) for h in H]
        return tuple((m_new[h], alpha[h] * carry[h][1] + pv[h]) for h in H)

    init = tuple((jnp.full((1, Q), -1e30, F32), jnp.zeros((HEAD_DIM + ONES_ROWS, Q), F32)) for _ in H)
    carry = lax.fori_loop(0, i - 1, lambda j, c: tile(j, c, "far"), init)
    carry = tile(jnp.maximum(i - 1, 0), carry, "prev")
    carry = tile(i, carry, "own")
    for h in H:
        acc = carry[h][1]
        o_ref[h * HEAD_DIM:(h + 1) * HEAD_DIM, :] = (acc[:HEAD_DIM] / acc[HEAD_DIM:HEAD_DIM + 1]).astype(o_ref.dtype)


def _moba_kernel(qt_ref, k_ref, vt_ref, bias_ref, o_ref, kmean_ref, sel_ref, *, nb):
    i = pl.program_id(1)
    Q = ATT_BLOCK

    @pl.when(i == 0)
    def _():
        means = [jnp.mean(k_ref[j * Q:(j + 1) * Q, :].astype(F32), axis=0, keepdims=True) for j in range(nb)]
        kmean = jnp.concatenate(means * N_HEADS, axis=0)
        shape = (N_HEADS * nb, BRANCH_W)
        row_head = lax.broadcasted_iota(jnp.int32, shape, 0) // nb
        lane_head = lax.broadcasted_iota(jnp.int32, shape, 1) // HEAD_DIM
        rest = jnp.where(row_head == lane_head, kmean, 0.0)
        for piece in range(3):
            part = rest.astype(BF16)
            kmean_ref[piece] = part
            rest = rest - part.astype(F32)

    qt = qt_ref[...]
    gate = _mm(kmean_ref[0], qt) + _mm(kmean_ref[1], qt) + _mm(kmean_ref[2], qt)
    gate = gate.reshape(N_HEADS, nb, Q)
    brow = lax.broadcasted_iota(jnp.int32, (N_HEADS, nb, Q), 1)
    n_sel = min(MOBA_TOPK, nb - 1)
    rank = jnp.zeros((N_HEADS, nb, Q), F32)
    for jp in range(nb):
        gj = gate[:, jp:jp + 1, :]
        ahead = (gj > gate) | ((gj == gate) & (jp < brow))
        rank = rank + jnp.where(ahead & (jp < i), 1.0, 0.0)
    sel_ref[...] = jnp.where((brow < i) & (rank < n_sel), 1.0, 0.0)

    def query_mask_fn(h, j, kind):
        return None if kind == "own" else sel_ref[h, pl.ds(j, 1), :] > 0.0

    _masked_attention(i, qt_ref, k_ref, vt_ref, bias_ref, o_ref, query_mask_fn=query_mask_fn)


def _moba_call(mqt, mk, mvt, bias_t, B, S):
    nb = S // ATT_BLOCK
    T = B * S
    qspec = pl.BlockSpec((BRANCH_W, ATT_BLOCK), lambda b, i: (0, b * nb + i))
    return pl.pallas_call(
        functools.partial(_moba_kernel, nb=nb),
        grid=(B, nb),
        in_specs=[qspec,
                  pl.BlockSpec((S, BRANCH_W), lambda b, i: (b, 0)),
                  pl.BlockSpec((BRANCH_W, S), lambda b, i: (0, b)),
                  _resident(bias_t.shape)],
        out_specs=qspec,
        out_shape=jax.ShapeDtypeStruct((BRANCH_W, T), BF16),
        scratch_shapes=[pltpu.VMEM((3, N_HEADS * nb, BRANCH_W), BF16),
                        pltpu.VMEM((N_HEADS, nb, ATT_BLOCK), F32)],
        compiler_params=pltpu.CompilerParams(dimension_semantics=("arbitrary", "arbitrary"),
                                             vmem_limit_bytes=VMEM_LIMIT),
        name="moba",
    )(mqt, mk, mvt, bias_t)


def _dsa_kernel(qt_ref, k_ref, vt_ref, qit_ref, wt_ref, kx_ref, bias_ref, perm_ref, o_ref,
                kx3_ref, keys_ref, khi_ref, klo_ref, mask_ref, cut_ref, *, n_keep, n_idx_bits, nq):
    i = pl.program_id(1)
    Q = ATT_BLOCK
    krow = lax.broadcasted_iota(jnp.int32, (Q, Q), 0)
    qcol = lax.broadcasted_iota(jnp.int32, (Q, Q), 1)

    def blk(j):
        return pl.ds(pl.multiple_of(j * Q, Q), Q)

    def split(t):
        hi = t.astype(BF16)
        return hi, (t - hi.astype(F32)).astype(BF16)

    @pl.when(i == 0)
    def _():
        for j in range(nq):
            hi, lo = split(kx_ref[j * Q:(j + 1) * Q, :])
            kx3_ref[j * Q:(j + 1) * Q, :] = (_mm(hi, perm_ref[0]) + _mm(lo, perm_ref[1])).astype(BF16)

    rhs = []
    for h in range(IDX_HEADS):
        hi, lo = split(qit_ref[h * IDX_DIM:(h + 1) * IDX_DIM, :])
        rhs.append(jnp.concatenate([hi, hi, lo, jnp.zeros_like(hi)], axis=0))

    def score_body(j, _):
        kt3 = kx3_ref[blk(j), :]
        acc = jnp.zeros((Q, Q), F32)
        for h in range(IDX_HEADS):
            acc = acc + jnp.maximum(_mm(kt3, rhs[h]), 0.0) * wt_ref[h:h + 1, :]
        acc = jnp.where(acc == 0.0, 0.0, acc)
        bits = pltpu.bitcast(acc, jnp.int32)
        key = jnp.where(bits < 0, bits ^ jnp.int32(0x7FFFFFFF), bits)
        causal = (j * Q + krow) <= (i * Q + qcol)
        key = jnp.where(causal, key, jnp.int32(INT_MIN))
        keys_ref[blk(j), :] = key
        khi_ref[blk(j), :] = (key >> 16).astype(jnp.int16)
        klo_ref[blk(j), :] = ((key & jnp.int32(0xFFFF)) - 32768).astype(jnp.int16)
        return 0

    lax.fori_loop(0, i + 1, score_body, 0)

    def count16(ref, pred):
        rows = 16
        def body(j, acc):
            hit = jnp.where(pred(ref[blk(j), :]), jnp.int16(1), jnp.int16(0))
            parts = [hit[rows * r:rows * (r + 1), :] for r in range(Q // rows)]
            while len(parts) > 1:
                parts = [x + y for x, y in zip(parts[::2], parts[1::2])]
            return acc + parts[0]
        acc = lax.fori_loop(0, i + 1, body, jnp.zeros((rows, Q), jnp.int16))
        return jnp.sum(acc.astype(jnp.int32), axis=0, keepdims=True)

    def bisect16(ref, target):
        def body(it, v):
            cand = v + jnp.left_shift(jnp.int32(1), 15 - it)
            c16 = cand.astype(jnp.int16)
            return jnp.where(count16(ref, lambda x: x >= c16) >= target, cand, v)
        return lax.fori_loop(0, 16, body, jnp.full((1, Q), -32768, jnp.int32))

    def count(pred):
        def body(j, acc8):
            hit = jnp.where(pred(keys_ref[blk(j), :], j * Q), 1.0, 0.0)
            parts = [hit[8 * r:8 * r + 8, :] for r in range(Q // 8)]
            while len(parts) > 1:
                parts = [x + y for x, y in zip(parts[::2], parts[1::2])]
            return acc8 + parts[0]
        acc8 = lax.fori_loop(0, i + 1, body, jnp.zeros((8, Q), F32))
        return jnp.sum(acc8, axis=0, keepdims=True)

    thr_hi = bisect16(khi_ref, n_keep)
    hi16 = thr_hi.astype(jnp.int16)
    need_lo = n_keep - count16(khi_ref, lambda x: x > hi16)

    def lo_mask_body(j, _):
        klo_ref[blk(j), :] = jnp.where(khi_ref[blk(j), :] == hi16, klo_ref[blk(j), :], jnp.int16(-32768))
        return 0

    lax.fori_loop(0, i + 1, lo_mask_body, 0)
    thr = thr_hi * 65536 + (bisect16(klo_ref, need_lo) + 32768)

    n_gt = count(lambda kb, k0: kb > thr)
    n_ge = count(lambda kb, k0: kb >= thr)
    need = n_keep - n_gt
    excess = jnp.where((n_ge - n_gt > need) & (thr > jnp.int32(INT_MIN)), 1.0, 0.0)
    cut_ref[...] = jnp.full(cut_ref.shape, 2 ** 30, jnp.int32)

    @pl.when(jnp.max(excess) > 0.0)
    def _():
        def cut_body(it, cut):
            cand = cut + jnp.left_shift(jnp.int32(1), n_idx_bits - 1 - it)
            cnt = count(lambda kb, k0: (kb == thr) & ((k0 + krow) < cand))
            return jnp.where(cnt <= need, cand, cut)
        cut_ref[0:1, :] = lax.fori_loop(0, n_idx_bits, cut_body, jnp.zeros((1, Q), jnp.int32))

    cut = cut_ref[0:1, :]

    def mask_body(j, _):
        kb = keys_ref[blk(j), :]
        keep = (kb > thr) | ((kb == thr) & ((j * Q + krow) < cut) & (kb > jnp.int32(INT_MIN)))
        mask_ref[blk(j), :] = jnp.where(keep, 0.0, NEG_INF)
        return 0

    lax.fori_loop(0, i + 1, mask_body, 0)

    def mask_fn(h, j, st, kind):
        st = st + mask_ref[blk(j), :]
        return jnp.where(i >= 1, st, NEG_INF) if kind == "prev" else st

    _masked_attention(i, qt_ref, k_ref, vt_ref, bias_ref, o_ref, mask_fn)


def _dsa_call(dqt, dk, dvt, diqt, dik, diwt, bias_t, perm, B, S):
    nq = S // ATT_BLOCK
    T = B * S
    n_keep = min(DSA_TOPK_MAX, S // 4)
    qcol = lambda c: pl.BlockSpec((c, ATT_BLOCK), lambda b, i: (0, b * nq + i))
    return pl.pallas_call(
        functools.partial(_dsa_kernel, n_keep=n_keep, n_idx_bits=int(math.log2(S)) + 1, nq=nq),
        grid=(B, nq),
        in_specs=[qcol(BRANCH_W),
                  pl.BlockSpec((S, BRANCH_W), lambda b, i: (b, 0)),
                  pl.BlockSpec((BRANCH_W, S), lambda b, i: (0, b)),
                  qcol(IDX_HEADS * IDX_DIM), qcol(IDX_HEADS),
                  pl.BlockSpec((S, 128), lambda b, i: (b, 0)),
                  _resident(bias_t.shape), _resident(perm.shape)],
        out_specs=qcol(BRANCH_W),
        out_shape=jax.ShapeDtypeStruct((BRANCH_W, T), BF16),
        scratch_shapes=[pltpu.VMEM((S, 128), BF16), pltpu.VMEM((S, ATT_BLOCK), jnp.int32),
                        pltpu.VMEM((S, ATT_BLOCK), jnp.int16), pltpu.VMEM((S, ATT_BLOCK), jnp.int16),
                        pltpu.VMEM((S, ATT_BLOCK), F32), pltpu.VMEM((8, ATT_BLOCK), jnp.int32)],
        compiler_params=pltpu.CompilerParams(dimension_semantics=("arbitrary", "arbitrary"),
                                             vmem_limit_bytes=VMEM_LIMIT),
        name="dsa",
    )(dqt, dk, dvt, diqt, diwt, dik, bias_t, perm)


def _dot3(x, w_ref):
    hi = x.astype(BF16)
    lo = (x - hi.astype(F32)).astype(BF16)
    return _mm(hi, w_ref[0]) + _mm(lo, w_ref[0]) + _mm(hi, w_ref[1])


def _rwkv_prologue(p, last, mu, vec_ref, w2_ref, a2_ref, g2_ref, gmat, vres_refs):
    rid = lax.broadcasted_iota(jnp.int32, p.shape, 0)
    shifted = jnp.where(rid == 0, last, pltpu.roll(p, 1, 0))
    p = p + (shifted - p) * mu
    r = p[:, 0:512]
    k = p[:, 512:1024]
    v = p[:, 1024:1536]
    wd = p[:, 1536:1664]
    ad = p[:, 1664:1792]
    gd = p[:, 1792:2048]
    w0, a0, k_k, k_a = vec_ref[0:1, :], vec_ref[1:2, :], vec_ref[2:3, :], vec_ref[3:4, :]
    z = -(w0 + _dot3(jnp.tanh(wd), w2_ref))
    softplus = jnp.maximum(z, 0.0) + jnp.log(1.0 + jnp.exp(-jnp.abs(z)))
    lw = -jnp.exp(-softplus - 0.5)
    a = jax.nn.sigmoid(a0 + _dot3(ad, a2_ref))
    g = _mm(jax.nn.sigmoid(gd).astype(BF16), g2_ref[...])
    if vres_refs is not None:
        vf_ref, v0_ref, va_ref, vb_ref = vres_refs
        low = _mm(v.astype(BF16), va_ref[...])
        mix = jax.nn.sigmoid(v0_ref[...] + _mm(low.astype(BF16), vb_ref[...]))
        v = v + (vf_ref[...] - v) * mix
    kk = k * k_k
    nrm = jnp.sqrt(_group_sum(kk * kk, gmat))
    kk = kk / jnp.maximum(nrm, 1e-12)
    return r, lw, k * (1.0 + (a - 1.0) * k_a), v, -kk, kk * a, g


def _rwkv_kernel(*refs, n_chunks, has_vres):
    (rw_ref, prev_ref, mu_ref, pre_vec_ref, w2_ref, a2_ref, g2_ref, gmat_ref, vec_ref, tri_ref), rest = refs[:10], refs[10:]
    if has_vres:
        vres_refs, (o_ref, state_ref, oraw_ref) = rest[:4], rest[4:]
    else:
        vres_refs, (o_ref, vfirst_ref, state_ref, oraw_ref) = None, rest
    C = RWKV_CHUNK
    N = HEAD_DIM
    bf = lambda t: t.astype(BF16)
    first_tile = pl.program_id(1) == 0

    @pl.when(first_tile)
    def _():
        state_ref[...] = jnp.zeros_like(state_ref)

    last = jnp.where(first_tile, 0.0, prev_ref[7:8, :])
    r, lw, k, v, a, b, g = _rwkv_prologue(rw_ref[...], last, mu_ref[...], pre_vec_ref, w2_ref, a2_ref,
                                          g2_ref, gmat_ref[...], vres_refs)
    if not has_vres:
        vfirst_ref[...] = v
    cum = _mm(tri_ref[...], lw, HIGHEST)
    e_neg = jnp.exp(-cum)
    at = a * jnp.exp(cum - lw)
    rt = r * jnp.exp(cum)
    bt = b * e_neg
    kt = k * e_neg
    bp, kp, p_end = [], [], []
    for c in range(n_chunks):
        rows = slice(c * C, (c + 1) * C)
        cum_last = cum[(c + 1) * C - 1:(c + 1) * C, :]
        e_end = jnp.exp(cum_last - cum[rows])
        bp.append(b[rows] * e_end)
        kp.append(k[rows] * e_end)
        p_end.append(jnp.exp(cum_last))

    rr = lax.broadcasted_iota(jnp.int32, (C, C), 0)
    cc = lax.broadcasted_iota(jnp.int32, (C, C), 1)
    strict = rr > cc
    incl = rr >= cc
    eye = jnp.where(rr == cc, 1.0, 0.0)
    pairs = [(c, h) for c in range(n_chunks) for h in range(N_HEADS)]
    sl = lambda t, c, h: t[c * C:(c + 1) * C, h * N:(h + 1) * N]
    hl = lambda t, h: t[:, h * N:(h + 1) * N]
    each = lambda f: {p: f(*p) for p in pairs}

    aall = each(lambda c, h: _nt(bf(jnp.concatenate([sl(at, c, h), sl(rt, c, h)], axis=0)),
                                 bf(jnp.concatenate([sl(bt, c, h), sl(kt, c, h)], axis=0))))
    vb = each(lambda c, h: bf(sl(v, c, h)))
    akv = each(lambda c, h: _mm(bf(jnp.where(strict, aall[c, h][:C, C:], 0.0)), vb[c, h]))
    o_loc = each(lambda c, h: _mm(bf(jnp.where(incl, aall[c, h][C:, C:], 0.0)), vb[c, h]))
    a_rb = each(lambda c, h: bf(jnp.where(incl, aall[c, h][C:, :C], 0.0)))
    x = each(lambda c, h: jnp.where(strict, aall[c, h][:C, :C], 0.0))
    tinv = each(lambda c, h: eye + x[c, h])
    for _ in range(int(math.log2(C)) - 1):
        x = each(lambda c, h: _mm(bf(x[c, h]), bf(x[c, h])))
        tinv = each(lambda c, h: tinv[c, h] + _mm(bf(tinv[c, h]), bf(x[c, h])))
    w_t = each(lambda c, h: _mm(bf(tinv[c, h]), bf(sl(at, c, h))))
    u_loc = each(lambda c, h: _mm(bf(tinv[c, h]), bf(akv[c, h])))

    heads = range(N_HEADS)
    state = [state_ref[h] for h in heads]
    for c in range(n_chunks):
        xs = [_nt(bf(jnp.concatenate([w_t[c, h], sl(rt, c, h)], axis=0)), bf(state[h])) for h in heads]
        u = [xs[h][:C] + u_loc[c, h] for h in heads]
        upd = [_tn(bf(jnp.concatenate([u[h], sl(v, c, h)], axis=0)),
                   bf(jnp.concatenate([hl(bp[c], h), hl(kp[c], h)], axis=0))) for h in heads]
        state = [state[h] * hl(p_end[c], h) + upd[h] for h in heads]
        o = [xs[h][C:] + _mm(a_rb[c, h], bf(u[h])) + o_loc[c, h] for h in heads]
        for h in heads:
            oraw_ref[c * C:(c + 1) * C, h * N:(h + 1) * N] = o[h]
    for h in heads:
        state_ref[h] = state[h]

    gmat = gmat_ref[...]
    o = oraw_ref[...]
    ln_w, ln_b, r_k = vec_ref[0:1, :], vec_ref[1:2, :], vec_ref[2:3, :]
    mean = _group_sum(o, gmat) * (1.0 / N)
    d = o - mean
    var = _group_sum(d * d, gmat) * (1.0 / N)
    o = d * lax.rsqrt(var + RWKV_GN_EPS) * ln_w + ln_b
    bonus = _group_sum(r * k * r_k, gmat)
    o_ref[...] = ((o + bonus * v) * g).astype(o_ref.dtype)


def _chunk_cumsum_matrix(tt):
    t = np.arange(tt)
    same_chunk = (t[:, None] // RWKV_CHUNK) == (t[None, :] // RWKV_CHUNK)
    return (same_chunk & (t[:, None] >= t[None, :])).astype(np.float32)


def _rwkv_call(rw, mu, pre_vec, w2, a2, g2, gmat, vec, tri, B, S, vres=None, tt=RWKV_TILE):
    T = B * S
    nt = S // tt
    row = lambda w: pl.BlockSpec((tt, w), lambda bi, t: (bi * nt + t, 0))
    prev = pl.BlockSpec((8, RW_W), lambda bi, t: (jnp.maximum((bi * nt + t) * (tt // 8) - 1, 0), 0))
    ins = [rw, rw, mu, pre_vec, w2, a2, g2, gmat, vec, tri]
    specs = [row(RW_W), prev] + [_resident(t.shape) for t in ins[2:]]
    out_specs, out_shape = [row(BRANCH_W)], [jax.ShapeDtypeStruct((T, BRANCH_W), BF16)]
    if vres is not None:
        vf, v0, va, vb = vres
        ins += [vf, v0, va, vb]
        specs += [row(BRANCH_W), _resident(v0.shape), _resident(va.shape), _resident(vb.shape)]
    else:
        out_specs.append(row(BRANCH_W))
        out_shape.append(jax.ShapeDtypeStruct((T, BRANCH_W), F32))
    outs = pl.pallas_call(
        functools.partial(_rwkv_kernel, n_chunks=tt // RWKV_CHUNK, has_vres=vres is not None),
        grid=(B, nt),
        in_specs=specs,
        out_specs=out_specs,
        out_shape=out_shape,
        scratch_shapes=[pltpu.VMEM((N_HEADS, HEAD_DIM, HEAD_DIM), F32),
                        pltpu.VMEM((tt, BRANCH_W), F32)],
        compiler_params=pltpu.CompilerParams(dimension_semantics=("arbitrary", "arbitrary"),
                                             vmem_limit_bytes=VMEM_LIMIT),
        name="rwkv",
    )(*ins)
    return (outs[0], None) if vres is not None else (outs[0], outs[1])


def _merge_ffn_kernel(x_ref, gate_ref, oa_ref, ob_ref, oc_ref, wbr_ref, wo_ref, nf_ref,
                      wfi_ref, wfo_ref, o_ref, act_ref, *, ff_chunk):
    mix = None
    for n, br in enumerate((oa_ref, ob_ref, oc_ref)):
        y = _mm(br[...], wbr_ref[n]) if n == 1 else _tn(br[...], wbr_ref[n])
        gt = jax.nn.sigmoid(gate_ref[:, n * D_MODEL:(n + 1) * D_MODEL].astype(F32))
        mix = gt * y if mix is None else mix + gt * y
    x = x_ref[...] + _mm(mix.astype(BF16), wo_ref[...])
    ms = jnp.mean(x * x, axis=-1, keepdims=True)
    h = (x * lax.rsqrt(ms + NORM_EPS) * nf_ref[...]).astype(BF16)
    for c in range(0, D_FF, ff_chunk):
        fg = _mm(h, wfi_ref[:, c:c + ff_chunk])
        fu = _mm(h, wfi_ref[:, D_FF + c:D_FF + c + ff_chunk])
        act_ref[:, c:c + ff_chunk] = (fg * jax.nn.sigmoid(fg) * fu).astype(BF16)
    o_ref[...] = x + _mm(act_ref[...], wfo_ref[...])


def _merge_ffn_call(x2, gate, oa, ob, oc, wbr, wo, nf, wfi, wfo, tm=512, ff_chunk=256):
    T = x2.shape[0]
    row = lambda w: pl.BlockSpec((tm, w), lambda i: (i, 0))
    col = pl.BlockSpec((BRANCH_W, tm), lambda i: (0, i))
    return pl.pallas_call(
        functools.partial(_merge_ffn_kernel, ff_chunk=ff_chunk),
        grid=(T // tm,),
        in_specs=[row(D_MODEL), row(3 * D_MODEL), col, row(BRANCH_W), col,
                  _resident(wbr.shape), _resident(wo.shape), _resident(nf.shape),
                  _resident(wfi.shape), _resident(wfo.shape)],
        out_specs=row(D_MODEL),
        out_shape=jax.ShapeDtypeStruct((T, D_MODEL), F32),
        scratch_shapes=[pltpu.VMEM((tm, D_FF), BF16)],
        compiler_params=pltpu.CompilerParams(dimension_semantics=("arbitrary",),
                                             vmem_limit_bytes=VMEM_LIMIT),
        name="merge_ffn",
    )(x2, gate, oa, ob, oc, wbr, wo, nf, wfi, wfo)


def _bucket_table():
    n = np.arange(2 * ATT_BLOCK)
    max_exact = REL_BUCKETS // 2
    nf = np.maximum(n, 1).astype(np.float32)
    large = max_exact + (np.log(nf / max_exact) / math.log(REL_MAX_DIST / max_exact)
                         * (REL_BUCKETS - max_exact)).astype(np.int32)
    large = np.minimum(large, REL_BUCKETS - 1)
    return np.where(n < max_exact, n, large).astype(np.int32)


def _bias_strips(rel_bias):
    period = 3 * ATT_BLOCK
    u = np.arange(period)
    dist = ATT_BLOCK + np.where(u < ATT_BLOCK, u, u - period)
    vec = jnp.where(jnp.asarray(dist >= 0)[None, :], rel_bias[_bucket_table()[np.maximum(dist, 0)]].T, NEG_INF)
    heads = vec.shape[0]
    skew = jnp.tile(vec, (1, 2 * ATT_BLOCK))[:, :2 * ATT_BLOCK * (period - 1)]
    return skew.reshape(heads, 2 * ATT_BLOCK, period - 1)[:, :, :ATT_BLOCK]


def _pad_cols(t, width):
    return jnp.pad(t, ((0, 0), (0, width - t.shape[1])))


def _pad_rows(t, height):
    return jnp.pad(t, ((0, height - t.shape[0]), (0, 0)))


def _hi_lo(w):
    hi = w.astype(BF16)
    return jnp.stack([hi, (w - hi.astype(F32)).astype(BF16)])


def _split_w_in(w):
    o = 0
    def take(n):
        nonlocal o
        s = w[:, o:o + n]
        o += n
        return s
    mq, mk, mv = take(512), take(512), take(512)
    r, wd, k, v, ad, gd = take(512), take(LORA_DECAY), take(512), take(512), take(LORA_AAA), take(LORA_GATE)
    dq, dc, diq, dik, diw = take(512), take(DSA_KV_RANK), take(256), take(IDX_DIM), take(IDX_HEADS)
    gate = take(3 * D_MODEL)
    w_tm = jnp.concatenate([gate, mk, r, k, v, _pad_cols(wd, 128), _pad_cols(ad, 128), _pad_cols(gd, 256),
                            dc, _pad_cols(dik, 128)], axis=1)
    w_cm = jnp.concatenate([mq, mv, dq, diq, diw], axis=1).T
    return w_tm.astype(BF16), w_cm.astype(BF16)


def _index_key_perm():
    p = np.zeros((2, 128, 128), np.float32)
    c = np.arange(IDX_DIM)
    p[0, c, c] = 1.0
    p[0, c, 2 * IDX_DIM + c] = 1.0
    p[1, c, IDX_DIM + c] = 1.0
    return jnp.asarray(p, BF16)


def _permute_mu(mu):
    r, wd, k, v, ad, gd = jnp.split(mu, np.cumsum([512, LORA_DECAY, 512, 512, LORA_AAA])[:].tolist())
    z = lambda n: jnp.zeros((n,), F32)
    return jnp.concatenate([r, k, v, wd, z(64), ad, z(64), gd, z(256 - LORA_GATE)])[None, :]


def kernel(x, w_in, norm_mix, norm_ffn, qk_norm, rel_bias, rwkv_mu, rwkv_w0, rwkv_w2, rwkv_a0, rwkv_a2, rwkv_g2, rwkv_kk, rwkv_ka, rwkv_rk, rwkv_ln_w, rwkv_ln_b, rwkv_v0, rwkv_va, rwkv_vb, dsa_kv_norm, dsa_kv_up, w_branch, w_o, w_ffn_in, w_ffn_out):
    B, S, D = x.shape
    depth = w_in.shape[0]
    T = B * S
    strips = _bias_strips(rel_bias * LOG2E)
    moba_bias, dsa_bias = strips[:N_HEADS], strips[N_HEADS:]
    head_of_lane = np.arange(BRANCH_W) // HEAD_DIM
    gmat = jnp.asarray(head_of_lane[:, None] == head_of_lane[None, :], BF16)
    tri = jnp.asarray(_chunk_cumsum_matrix(RWKV_TILE))
    perm = _index_key_perm()
    x2 = x.reshape(T, D)
    v_first = None
    for l in range(depth):
        w_tm, w_cm = _split_w_in(w_in[l])
        per_channel = lambda g: jnp.tile(g, N_HEADS)
        gq = jnp.stack([per_channel(qk_norm[l, 0]), per_channel(qk_norm[l, 2])]) * (LOG2E * HEAD_DIM ** -0.5)
        gq = jnp.broadcast_to(gq[:, :, None], (2, BRANCH_W, ATT_BLOCK))
        gk = jnp.stack([per_channel(qk_norm[l, 1]), per_channel(qk_norm[l, 3])])
        kv_up = dsa_kv_up[l].astype(BF16)
        (gate, mqt, mk, mvt, rw, dqt, dk, dvt, diqt, dik, diwt) = _proj_call(
            x2, norm_mix[l][None, :], w_tm, w_cm, dsa_kv_norm[l][None, :],
            kv_up[:, :BRANCH_W], kv_up[:, BRANCH_W:].T, gk, gq, gmat)
        o_moba = _moba_call(mqt, mk, mvt, moba_bias, B, S)
        o_dsa = _dsa_call(dqt, dk, dvt, diqt, dik, diwt, dsa_bias, perm, B, S)
        vec = jnp.stack([rwkv_w0[l], rwkv_a0[l], rwkv_kk[l], rwkv_ka[l]])
        vres = None
        if l > 0:
            vres = (v_first, rwkv_v0[l - 1][None, :], _pad_cols(rwkv_va[l - 1], 128).astype(BF16),
                    _pad_rows(rwkv_vb[l - 1], 128).astype(BF16))
        vec2 = jnp.stack([rwkv_ln_w[l], rwkv_ln_b[l], rwkv_rk[l].reshape(-1)])
        o_rwkv, v_new = _rwkv_call(
            rw, _permute_mu(rwkv_mu[l]), vec, _hi_lo(_pad_rows(rwkv_w2[l], 128)), _hi_lo(_pad_rows(rwkv_a2[l], 128)),
            _pad_rows(rwkv_g2[l], 256).astype(BF16), gmat, vec2, tri, B, S, vres)
        if l == 0:
            v_first = v_new
        x2 = _merge_ffn_call(x2, gate, o_moba, o_rwkv, o_dsa, w_branch[l].astype(BF16),
                             w_o[l].astype(BF16), norm_ffn[l][None, :],
                             w_ffn_in[l].astype(BF16), w_ffn_out[l].astype(BF16))
    return x2.reshape(B, S, D)
```

```python
import functools
import math

import numpy as np
import jax
import jax.numpy as jnp
from jax import lax
from jax.experimental import pallas as pl
from jax.experimental.pallas import tpu as pltpu

F32 = jnp.float32
BF16 = jnp.bfloat16
HIGHEST = lax.Precision.HIGHEST

D_MODEL = 1024
HEAD_DIM = 64
BRANCH_W = 512
N_HEADS = BRANCH_W // HEAD_DIM
MOBA_BLOCK = 256
MOBA_TOPK = 3
LORA_DECAY = 64
LORA_AAA = 64
LORA_GATE = 160
LORA_MV = 32
RWKV_GN_EPS = 64e-5
DSA_KV_RANK = 256
IDX_HEADS = 8
IDX_DIM = 32
DSA_TOPK_MAX = 256
REL_BUCKETS = 32
REL_MAX_DIST = 128
D_FF = 2816
NORM_EPS = 1e-6

ATT_BLOCK = 256
RWKV_CHUNK = 64
RWKV_TILE = 256
INT_MIN = -(2 ** 31)
NEG_INF = float("-inf")
LOG2E = math.log2(math.e)
VMEM_LIMIT = 52 * 1024 * 1024

TM_GATE, TM_MK, TM_RW, TM_DC, TM_DIK, N_TM = 0, 3072, 3584, 5632, 5888, 6016
CM_MQ, CM_MV, CM_DQ, CM_DIQ, CM_DIW, N_CM = 0, 512, 1024, 1536, 1792, 1800
RW_W = 2048


def _nt(a, b, precision=None):
    return lax.dot_general(a, b, (((1,), (1,)), ((), ())), precision=precision,
                           preferred_element_type=F32)


def _tn(a, b, precision=None):
    return lax.dot_general(a, b, (((0,), (0,)), ((), ())), precision=precision,
                           preferred_element_type=F32)


def _mm(a, b, precision=None):
    return jnp.dot(a, b, precision=precision, preferred_element_type=F32)


def _group_sum(t, gmat):
    hi = t.astype(BF16)
    lo = (t - hi.astype(F32)).astype(BF16)
    return _mm(hi, gmat) + _mm(lo, gmat)


def _head_rms(t, gain, gmat):
    ms = _mm((t * t).astype(BF16), gmat) * (1.0 / HEAD_DIM)
    return t * lax.rsqrt(ms + NORM_EPS) * gain


def _resident(shape):
    nd = len(shape)
    return pl.BlockSpec(shape, lambda *_: (0,) * nd, pipeline_mode=pl.Buffered(1))


def _head_rms_cm(t, gain):
    parts = []
    for h in range(N_HEADS):
        blk = t[h * HEAD_DIM:(h + 1) * HEAD_DIM, :]
        ms = jnp.mean(blk * blk, axis=0, keepdims=True)
        parts.append(blk * lax.rsqrt(ms + NORM_EPS))
    return jnp.concatenate(parts, axis=0) * gain


def _proj_kernel(x_ref, g_ref, wtm_ref, wcm_ref, kvn_ref, kvk_ref, kvvt_ref, qkn_ref, gq_ref, gmat_ref,
                 gate_o, mqt_o, mk_o, mvt_o, rw_o, dqt_o, dk_o, dvt_o, diqt_o, dik_o, diwt_o):
    x = x_ref[...]
    ms = jnp.mean(x * x, axis=-1, keepdims=True)
    h = (x * lax.rsqrt(ms + NORM_EPS) * g_ref[...]).astype(BF16)
    gmat = gmat_ref[...]

    def proj(off, width):
        return _mm(h, wtm_ref[:, off:off + width])

    def proj_t(off, width):
        return _nt(wcm_ref[off:off + width, :], h)

    gate_o[...] = proj(TM_GATE, 3 * D_MODEL).astype(BF16)
    mqt_o[...] = _head_rms_cm(proj_t(CM_MQ, BRANCH_W), gq_ref[0]).astype(BF16)
    mk_o[...] = _head_rms(proj(TM_MK, BRANCH_W), qkn_ref[0:1, :], gmat).astype(BF16)
    mvt_o[...] = proj_t(CM_MV, BRANCH_W).astype(BF16)
    rw_o[...] = proj(TM_RW, RW_W)
    dqt_o[...] = _head_rms_cm(proj_t(CM_DQ, BRANCH_W), gq_ref[1]).astype(BF16)
    dc = proj(TM_DC, DSA_KV_RANK)
    dcn = (dc * lax.rsqrt(jnp.mean(dc * dc, axis=-1, keepdims=True) + NORM_EPS) * kvn_ref[...]).astype(BF16)
    dk_o[...] = _head_rms(_mm(dcn, kvk_ref[...]), qkn_ref[1:2, :], gmat).astype(BF16)
    dvt_o[...] = _nt(kvvt_ref[...], dcn).astype(BF16)
    diqt_o[...] = proj_t(CM_DIQ, IDX_HEADS * IDX_DIM)
    dik_o[...] = proj(TM_DIK, 128)
    diwt_o[...] = proj_t(CM_DIW, IDX_HEADS)


def _proj_call(x2, g, w_tm, w_cm, kvn, kvk, kvvt, qkn, gq, gmat, tm=ATT_BLOCK):
    T = x2.shape[0]
    row = lambda w: pl.BlockSpec((tm, w), lambda i: (i, 0))
    col = lambda c: pl.BlockSpec((c, tm), lambda i: (0, i))
    tmaj = lambda w, d: (jax.ShapeDtypeStruct((T, w), d), row(w))
    cmaj = lambda c, d: (jax.ShapeDtypeStruct((c, T), d), col(c))
    outs = [tmaj(3 * D_MODEL, BF16), cmaj(BRANCH_W, BF16), tmaj(BRANCH_W, BF16), cmaj(BRANCH_W, BF16),
            tmaj(RW_W, F32), cmaj(BRANCH_W, BF16), tmaj(BRANCH_W, BF16), cmaj(BRANCH_W, BF16),
            cmaj(IDX_HEADS * IDX_DIM, F32), tmaj(128, F32), cmaj(IDX_HEADS, F32)]
    ins = [x2, g, w_tm, w_cm, kvn, kvk, kvvt, qkn, gq, gmat]
    return pl.pallas_call(
        _proj_kernel,
        grid=(T // tm,),
        in_specs=[row(D_MODEL)] + [_resident(t.shape) for t in ins[1:]],
        out_specs=[spec for _, spec in outs],
        out_shape=[shape for shape, _ in outs],
        compiler_params=pltpu.CompilerParams(dimension_semantics=("arbitrary",),
                                             vmem_limit_bytes=VMEM_LIMIT),
        name="proj",
    )(*ins)


ONES_ROWS = 16


def _masked_attention(i, qt_ref, k_ref, vt_ref, bias_ref, o_ref, mask_fn=None, query_mask_fn=None):
    Q = ATT_BLOCK
    H = range(N_HEADS)

    def blk(j):
        return pl.ds(pl.multiple_of(j * Q, Q), Q)

    prow = lax.broadcasted_iota(jnp.int32, (2 * HEAD_DIM, Q), 0)
    qm = []
    for p in range(N_HEADS // 2):
        pair = qt_ref[p * 2 * HEAD_DIM:(p + 1) * 2 * HEAD_DIM, :]
        zero = jnp.zeros_like(pair)
        qm += [jnp.where(prow < HEAD_DIM, pair, zero), jnp.where(prow >= HEAD_DIM, pair, zero)]
    ones = jnp.ones((ONES_ROWS, Q), BF16)

    def tile(j, carry, kind):
        kt = [k_ref[blk(j), p * 2 * HEAD_DIM:(p + 1) * 2 * HEAD_DIM] for p in range(N_HEADS // 2)]
        st = [_mm(kt[h // 2], qm[h]) for h in H]
        if kind == "prev":
            st = [st[h] + bias_ref[h, :Q, :] for h in H]
        elif kind == "own":
            st = [st[h] + bias_ref[h, Q:, :] for h in H]
        if mask_fn is not None:
            st = [mask_fn(h, j, st[h], kind) for h in H]
        tmax = [jnp.max(st[h], axis=0, keepdims=True) for h in H]
        attends = [None if query_mask_fn is None else query_mask_fn(h, j, kind) for h in H]
        tmax = [tmax[h] if attends[h] is None else jnp.where(attends[h], tmax[h], NEG_INF) for h in H]
        if kind == "far":
            far = [bias_ref[h, 0:1, Q - 1:Q] for h in H]
            tmax = [tmax[h] + far[h] for h in H]
        m_new = [jnp.maximum(carry[h][0], tmax[h]) for h in H]
        alpha = [jnp.exp2(carry[h][0] - m_new[h]) for h in H]
        sub = [m_new[h] - far[h] for h in H] if kind == "far" else m_new
        sub = [sub[h] if attends[h] is None else jnp.where(attends[h], sub[h], -NEG_INF) for h in H]
        p = [jnp.exp2(st[h] - sub[h]).astype(BF16) for h in H]
        vt = [jnp.concatenate([vt_ref[h * HEAD_DIM:(h + 1) * HEAD_DIM, blk(j)], ones], axis=0) for h in H]
        pv = [_mm(vt[h], p[h]) for h in H]
        return tuple((m_new[h], alpha[h] * carry[h][1] + pv[h]) for h in H)

    init = tuple((jnp.full((1, Q), -1e30, F32), jnp.zeros((HEAD_DIM + ONES_ROWS, Q), F32)) for _ in H)
    carry = lax.fori_loop(0, i - 1, lambda j, c: tile(j, c, "far"), init)
    carry = tile(jnp.maximum(i - 1, 0), carry, "prev")
    carry = tile(i, carry, "own")
    for h in H:
        acc = carry[h][1]
        o_ref[h * HEAD_DIM:(h + 1) * HEAD_DIM, :] = (acc[:HEAD_DIM] / acc[HEAD_DIM:HEAD_DIM + 1]).astype(o_ref.dtype)


def _moba_kernel(qt_ref, k_ref, vt_ref, bias_ref, o_ref, kmean_ref, sel_ref, *, nb):
    i = pl.program_id(1)
    Q = ATT_BLOCK

    @pl.when(i == 0)
    def _():
        means = [jnp.mean(k_ref[j * Q:(j + 1) * Q, :].astype(F32), axis=0, keepdims=True) for j in range(nb)]
        kmean = jnp.concatenate(means * N_HEADS, axis=0)
        shape = (N_HEADS * nb, BRANCH_W)
        row_head = lax.broadcasted_iota(jnp.int32, shape, 0) // nb
        lane_head = lax.broadcasted_iota(jnp.int32, shape, 1) // HEAD_DIM
        rest = jnp.where(row_head == lane_head, kmean, 0.0)
        for piece in range(3):
            part = rest.astype(BF16)
            kmean_ref[piece] = part
            rest = rest - part.astype(F32)

    qt = qt_ref[...]
    gate = _mm(kmean_ref[0], qt) + _mm(kmean_ref[1], qt) + _mm(kmean_ref[2], qt)
    gate = gate.reshape(N_HEADS, nb, Q)
    brow = lax.broadcasted_iota(jnp.int32, (N_HEADS, nb, Q), 1)
    n_sel = min(MOBA_TOPK, nb - 1)
    rank = jnp.zeros((N_HEADS, nb, Q), F32)
    for jp in range(nb):
        gj = gate[:, jp:jp + 1, :]
        ahead = (gj > gate) | ((gj == gate) & (jp < brow))
        rank = rank + jnp.where(ahead & (jp < i), 1.0, 0.0)
    sel_ref[...] = jnp.where((brow < i) & (rank < n_sel), 1.0, 0.0)

    def query_mask_fn(h, j, kind):
        return None if kind == "own" else sel_ref[h, pl.ds(j, 1), :] > 0.0

    _masked_attention(i, qt_ref, k_ref, vt_ref, bias_ref, o_ref, query_mask_fn=query_mask_fn)


def _moba_call(mqt, mk, mvt, bias_t, B, S):
    nb = S // ATT_BLOCK
    T = B * S
    qspec = pl.BlockSpec((BRANCH_W, ATT_BLOCK), lambda b, i: (0, b * nb + i))
    return pl.pallas_call(
        functools.partial(_moba_kernel, nb=nb),
        grid=(B, nb),
        in_specs=[qspec,
                  pl.BlockSpec((S, BRANCH_W), lambda b, i: (b, 0)),
                  pl.BlockSpec((BRANCH_W, S), lambda b, i: (0, b)),
                  _resident(bias_t.shape)],
        out_specs=qspec,
        out_shape=jax.ShapeDtypeStruct((BRANCH_W, T), BF16),
        scratch_shapes=[pltpu.VMEM((3, N_HEADS * nb, BRANCH_W), BF16),
                        pltpu.VMEM((N_HEADS, nb, ATT_BLOCK), F32)],
        compiler_params=pltpu.CompilerParams(dimension_semantics=("arbitrary", "arbitrary"),
                                             vmem_limit_bytes=VMEM_LIMIT),
        name="moba",
    )(mqt, mk, mvt, bias_t)


def _dsa_kernel(qt_ref, k_ref, vt_ref, qit_ref, wt_ref, kx_ref, bias_ref, perm_ref, o_ref,
                kx3_ref, keys_ref, khi_ref, klo_ref, mask_ref, cut_ref, *, n_keep, n_idx_bits, nq):
    i = pl.program_id(1)
    Q = ATT_BLOCK
    krow = lax.broadcasted_iota(jnp.int32, (Q, Q), 0)
    qcol = lax.broadcasted_iota(jnp.int32, (Q, Q), 1)

    def blk(j):
        return pl.ds(pl.multiple_of(j * Q, Q), Q)

    def split(t):
        hi = t.astype(BF16)
        return hi, (t - hi.astype(F32)).astype(BF16)

    @pl.when(i == 0)
    def _():
        for j in range(nq):
            hi, lo = split(kx_ref[j * Q:(j + 1) * Q, :])
            kx3_ref[j * Q:(j + 1) * Q, :] = (_mm(hi, perm_ref[0]) + _mm(lo, perm_ref[1])).astype(BF16)

    rhs = []
    for h in range(IDX_HEADS):
        hi, lo = split(qit_ref[h * IDX_DIM:(h + 1) * IDX_DIM, :])
        rhs.append(jnp.concatenate([hi, hi, lo, jnp.zeros_like(hi)], axis=0))

    def score_body(j, _):
        kt3 = kx3_ref[blk(j), :]
        acc = jnp.zeros((Q, Q), F32)
        for h in range(IDX_HEADS):
            acc = acc + jnp.maximum(_mm(kt3, rhs[h]), 0.0) * wt_ref[h:h + 1, :]
        acc = jnp.where(acc == 0.0, 0.0, acc)
        bits = pltpu.bitcast(acc, jnp.int32)
        key = jnp.where(bits < 0, bits ^ jnp.int32(0x7FFFFFFF), bits)
        causal = (j * Q + krow) <= (i * Q + qcol)
        key = jnp.where(causal, key, jnp.int32(INT_MIN))
        keys_ref[blk(j), :] = key
        khi_ref[blk(j), :] = (key >> 16).astype(jnp.int16)
        klo_ref[blk(j), :] = ((key & jnp.int32(0xFFFF)) - 32768).astype(jnp.int16)
        return 0

    lax.fori_loop(0, i + 1, score_body, 0)

    def count16(ref, pred):
        rows = 16
        def body(j, acc):
            hit = jnp.where(pred(ref[blk(j), :]), jnp.int16(1), jnp.int16(0))
            parts = [hit[rows * r:rows * (r + 1), :] for r in range(Q // rows)]
            while len(parts) > 1:
                parts = [x + y for x, y in zip(parts[::2], parts[1::2])]
            return acc + parts[0]
        acc = lax.fori_loop(0, i + 1, body, jnp.zeros((rows, Q), jnp.int16))
        return jnp.sum(acc.astype(jnp.int32), axis=0, keepdims=True)

    def bisect16(ref, target):
        def body(it, v):
            cand = v + jnp.left_shift(jnp.int32(1), 15 - it)
            c16 = cand.astype(jnp.int16)
            return jnp.where(count16(ref, lambda x: x >= c16) >= target, cand, v)
        return lax.fori_loop(0, 16, body, jnp.full((1, Q), -32768, jnp.int32))

    def count(pred):
        def body(j, acc8):
            hit = jnp.where(pred(keys_ref[blk(j), :], j * Q), 1.0, 0.0)
            parts = [hit[8 * r:8 * r + 8, :] for r in range(Q // 8)]
            while len(parts) > 1:
                parts = [x + y for x, y in zip(parts[::2], parts[1::2])]
            return acc8 + parts[0]
        acc8 = lax.fori_loop(0, i + 1, body, jnp.zeros((8, Q), F32))
        return jnp.sum(acc8, axis=0, keepdims=True)

    thr_hi = bisect16(khi_ref, n_keep)
    hi16 = thr_hi.astype(jnp.int16)
    need_lo = n_keep - count16(khi_ref, lambda x: x > hi16)

    def lo_mask_body(j, _):
        klo_ref[blk(j), :] = jnp.where(khi_ref[blk(j), :] == hi16, klo_ref[blk(j), :], jnp.int16(-32768))
        return 0

    lax.fori_loop(0, i + 1, lo_mask_body, 0)
    thr = thr_hi * 65536 + (bisect16(klo_ref, need_lo) + 32768)

    n_gt = count(lambda kb, k0: kb > thr)
    n_ge = count(lambda kb, k0: kb >= thr)
    need = n_keep - n_gt
    excess = jnp.where((n_ge - n_gt > need) & (thr > jnp.int32(INT_MIN)), 1.0, 0.0)
    cut_ref[...] = jnp.full(cut_ref.shape, 2 ** 30, jnp.int32)

    @pl.when(jnp.max(excess) > 0.0)
    def _():
        def eq_index_body(j, _):
            idx = jnp.where(keys_ref[blk(j), :] == thr, j * Q + krow, jnp.int32(32767))
            klo_ref[blk(j), :] = idx.astype(jnp.int16)
            return 0

        lax.fori_loop(0, i + 1, eq_index_body, 0)

        def cut_body(it, cut):
            cand = cut + jnp.left_shift(jnp.int32(1), n_idx_bits - 1 - it)
            c16 = cand.astype(jnp.int16)
            cnt = count16(klo_ref, lambda x: x < c16)
            return jnp.where(cnt.astype(F32) <= need, cand, cut)
        cut_ref[0:1, :] = lax.fori_loop(0, n_idx_bits, cut_body, jnp.zeros((1, Q), jnp.int32))

    cut = cut_ref[0:1, :]

    def mask_body(j, _):
        kb = keys_ref[blk(j), :]
        keep = (kb > thr) | ((kb == thr) & ((j * Q + krow) < cut) & (kb > jnp.int32(INT_MIN)))
        mask_ref[blk(j), :] = jnp.where(keep, 0.0, NEG_INF)
        return 0

    lax.fori_loop(0, i + 1, mask_body, 0)

    def mask_fn(h, j, st, kind):
        return st + mask_ref[blk(j), :]

    def query_mask_fn(h, j, kind):
        return jnp.broadcast_to(i >= 1, (1, Q)) if kind == "prev" else None

    _masked_attention(i, qt_ref, k_ref, vt_ref, bias_ref, o_ref, mask_fn, query_mask_fn)


def _dsa_call(dqt, dk, dvt, diqt, dik, diwt, bias_t, perm, B, S):
    nq = S // ATT_BLOCK
    T = B * S
    n_keep = min(DSA_TOPK_MAX, S // 4)
    assert S % ATT_BLOCK == 0 and S < 2 ** 14
    qcol = lambda c: pl.BlockSpec((c, ATT_BLOCK), lambda b, i: (0, b * nq + i))
    return pl.pallas_call(
        functools.partial(_dsa_kernel, n_keep=n_keep, n_idx_bits=int(math.log2(S)) + 1, nq=nq),
        grid=(B, nq),
        in_specs=[qcol(BRANCH_W),
                  pl.BlockSpec((S, BRANCH_W), lambda b, i: (b, 0)),
                  pl.BlockSpec((BRANCH_W, S), lambda b, i: (0, b)),
                  qcol(IDX_HEADS * IDX_DIM), qcol(IDX_HEADS),
                  pl.BlockSpec((S, 128), lambda b, i: (b, 0)),
                  _resident(bias_t.shape), _resident(perm.shape)],
        out_specs=qcol(BRANCH_W),
        out_shape=jax.ShapeDtypeStruct((BRANCH_W, T), BF16),
        scratch_shapes=[pltpu.VMEM((S, 128), BF16), pltpu.VMEM((S, ATT_BLOCK), jnp.int32),
                        pltpu.VMEM((S, ATT_BLOCK), jnp.int16), pltpu.VMEM((S, ATT_BLOCK), jnp.int16),
                        pltpu.VMEM((S, ATT_BLOCK), F32), pltpu.VMEM((8, ATT_BLOCK), jnp.int32)],
        compiler_params=pltpu.CompilerParams(dimension_semantics=("arbitrary", "arbitrary"),
                                             vmem_limit_bytes=VMEM_LIMIT),
        name="dsa",
    )(dqt, dk, dvt, diqt, diwt, dik, bias_t, perm)


def _dot3(x, w_ref):
    hi = x.astype(BF16)
    lo = (x - hi.astype(F32)).astype(BF16)
    return _mm(hi, w_ref[0]) + _mm(lo, w_ref[0]) + _mm(hi, w_ref[1])


def _rwkv_prologue(p, last, mu, vec_ref, w2_ref, a2_ref, g2_ref, gmat, vres_refs):
    rid = lax.broadcasted_iota(jnp.int32, p.shape, 0)
    shifted = jnp.where(rid == 0, last, pltpu.roll(p, 1, 0))
    p = p + (shifted - p) * mu
    r = p[:, 0:512]
    k = p[:, 512:1024]
    v = p[:, 1024:1536]
    wd = p[:, 1536:1664]
    ad = p[:, 1664:1792]
    gd = p[:, 1792:2048]
    w0, a0, k_k, k_a = vec_ref[0:1, :], vec_ref[1:2, :], vec_ref[2:3, :], vec_ref[3:4, :]
    z = -(w0 + _dot3(jnp.tanh(wd), w2_ref))
    softplus = jnp.maximum(z, 0.0) + jnp.log(1.0 + jnp.exp(-jnp.abs(z)))
    lw = -jnp.exp(-softplus - 0.5)
    a = jax.nn.sigmoid(a0 + _dot3(ad, a2_ref))
    g = _mm(jax.nn.sigmoid(gd).astype(BF16), g2_ref[...])
    if vres_refs is not None:
        vf_ref, v0_ref, va_ref, vb_ref = vres_refs
        low = _mm(v.astype(BF16), va_ref[...])
        mix = jax.nn.sigmoid(v0_ref[...] + _mm(low.astype(BF16), vb_ref[...]))
        v = v + (vf_ref[...] - v) * mix
    kk = k * k_k
    nrm = jnp.sqrt(_group_sum(kk * kk, gmat))
    kk = kk / jnp.maximum(nrm, 1e-12)
    return r, lw, k * (1.0 + (a - 1.0) * k_a), v, -kk, kk * a, g


def _rwkv_kernel(*refs, n_chunks, has_vres):
    (rw_ref, prev_ref, mu_ref, pre_vec_ref, w2_ref, a2_ref, g2_ref, gmat_ref, vec_ref, tri_ref), rest = refs[:10], refs[10:]
    if has_vres:
        vres_refs, (o_ref, state_ref, oraw_ref) = rest[:4], rest[4:]
    else:
        vres_refs, (o_ref, vfirst_ref, state_ref, oraw_ref) = None, rest
    C = RWKV_CHUNK
    N = HEAD_DIM
    bf = lambda t: t.astype(BF16)
    first_tile = pl.program_id(1) == 0

    @pl.when(first_tile)
    def _():
        state_ref[...] = jnp.zeros_like(state_ref)

    last = jnp.where(first_tile, 0.0, prev_ref[7:8, :])
    r, lw, k, v, a, b, g = _rwkv_prologue(rw_ref[...], last, mu_ref[...], pre_vec_ref, w2_ref, a2_ref,
                                          g2_ref, gmat_ref[...], vres_refs)
    if not has_vres:
        vfirst_ref[...] = v
    cum = _mm(tri_ref[...], lw, HIGHEST)
    e_neg = jnp.exp(-cum)
    at = a * jnp.exp(cum - lw)
    rt = r * jnp.exp(cum)
    bt = b * e_neg
    kt = k * e_neg
    bp, kp, p_end = [], [], []
    for c in range(n_chunks):
        rows = slice(c * C, (c + 1) * C)
        cum_last = cum[(c + 1) * C - 1:(c + 1) * C, :]
        e_end = jnp.exp(cum_last - cum[rows])
        bp.append(b[rows] * e_end)
        kp.append(k[rows] * e_end)
        p_end.append(jnp.exp(cum_last))

    rr = lax.broadcasted_iota(jnp.int32, (C, C), 0)
    cc = lax.broadcasted_iota(jnp.int32, (C, C), 1)
    strict = rr > cc
    incl = rr >= cc
    eye = jnp.where(rr == cc, 1.0, 0.0)
    pairs = [(c, h) for c in range(n_chunks) for h in range(N_HEADS)]
    sl = lambda t, c, h: t[c * C:(c + 1) * C, h * N:(h + 1) * N]
    hl = lambda t, h: t[:, h * N:(h + 1) * N]
    each = lambda f: {p: f(*p) for p in pairs}

    aall = each(lambda c, h: _nt(bf(jnp.concatenate([sl(at, c, h), sl(rt, c, h)], axis=0)),
                                 bf(jnp.concatenate([sl(bt, c, h), sl(kt, c, h)], axis=0))))
    vb = each(lambda c, h: bf(sl(v, c, h)))
    akv = each(lambda c, h: _mm(bf(jnp.where(strict, aall[c, h][:C, C:], 0.0)), vb[c, h]))
    o_loc = each(lambda c, h: _mm(bf(jnp.where(incl, aall[c, h][C:, C:], 0.0)), vb[c, h]))
    a_rb = each(lambda c, h: bf(jnp.where(incl, aall[c, h][C:, :C], 0.0)))
    x = each(lambda c, h: jnp.where(strict, aall[c, h][:C, :C], 0.0))
    tinv = each(lambda c, h: eye + x[c, h])
    for _ in range(int(math.log2(C)) - 1):
        x = each(lambda c, h: _mm(bf(x[c, h]), bf(x[c, h])))
        tinv = each(lambda c, h: tinv[c, h] + _mm(bf(tinv[c, h]), bf(x[c, h])))
    w_t = each(lambda c, h: _mm(bf(tinv[c, h]), bf(sl(at, c, h))))
    u_loc = each(lambda c, h: _mm(bf(tinv[c, h]), bf(akv[c, h])))

    heads = range(N_HEADS)
    state = [state_ref[h] for h in heads]
    for c in range(n_chunks):
        xs = [_nt(bf(jnp.concatenate([w_t[c, h], sl(rt, c, h)], axis=0)), bf(state[h])) for h in heads]
        u = [xs[h][:C] + u_loc[c, h] for h in heads]
        upd = [_tn(bf(jnp.concatenate([u[h], sl(v, c, h)], axis=0)),
                   bf(jnp.concatenate([hl(bp[c], h), hl(kp[c], h)], axis=0))) for h in heads]
        state = [state[h] * hl(p_end[c], h) + upd[h] for h in heads]
        o = [xs[h][C:] + _mm(a_rb[c, h], bf(u[h])) + o_loc[c, h] for h in heads]
        for h in heads:
            oraw_ref[c * C:(c + 1) * C, h * N:(h + 1) * N] = o[h]
    for h in heads:
        state_ref[h] = state[h]

    gmat = gmat_ref[...]
    o = oraw_ref[...]
    ln_w, ln_b, r_k = vec_ref[0:1, :], vec_ref[1:2, :], vec_ref[2:3, :]
    mean = _group_sum(o, gmat) * (1.0 / N)
    d = o - mean
    var = _group_sum(d * d, gmat) * (1.0 / N)
    o = d * lax.rsqrt(var + RWKV_GN_EPS) * ln_w + ln_b
    bonus = _group_sum(r * k * r_k, gmat)
    o_ref[...] = ((o + bonus * v) * g).astype(o_ref.dtype)


def _chunk_cumsum_matrix(tt):
    t = np.arange(tt)
    same_chunk = (t[:, None] // RWKV_CHUNK) == (t[None, :] // RWKV_CHUNK)
    return (same_chunk & (t[:, None] >= t[None, :])).astype(np.float32)


def _rwkv_call(rw, mu, pre_vec, w2, a2, g2, gmat, vec, tri, B, S, vres=None, tt=RWKV_TILE):
    T = B * S
    nt = S // tt
    row = lambda w: pl.BlockSpec((tt, w), lambda bi, t: (bi * nt + t, 0))
    prev = pl.BlockSpec((8, RW_W), lambda bi, t: (jnp.maximum((bi * nt + t) * (tt // 8) - 1, 0), 0))
    ins = [rw, rw, mu, pre_vec, w2, a2, g2, gmat, vec, tri]
    specs = [row(RW_W), prev] + [_resident(t.shape) for t in ins[2:]]
    out_specs, out_shape = [row(BRANCH_W)], [jax.ShapeDtypeStruct((T, BRANCH_W), BF16)]
    if vres is not None:
        vf, v0, va, vb = vres
        ins += [vf, v0, va, vb]
        specs += [row(BRANCH_W), _resident(v0.shape), _resident(va.shape), _resident(vb.shape)]
    else:
        out_specs.append(row(BRANCH_W))
        out_shape.append(jax.ShapeDtypeStruct((T, BRANCH_W), F32))
    outs = pl.pallas_call(
        functools.partial(_rwkv_kernel, n_chunks=tt // RWKV_CHUNK, has_vres=vres is not None),
        grid=(B, nt),
        in_specs=specs,
        out_specs=out_specs,
        out_shape=out_shape,
        scratch_shapes=[pltpu.VMEM((N_HEADS, HEAD_DIM, HEAD_DIM), F32),
                        pltpu.VMEM((tt, BRANCH_W), F32)],
        compiler_params=pltpu.CompilerParams(dimension_semantics=("arbitrary", "arbitrary"),
                                             vmem_limit_bytes=VMEM_LIMIT),
        name="rwkv",
    )(*ins)
    return (outs[0], None) if vres is not None else (outs[0], outs[1])


def _merge_ffn_kernel(x_ref, gate_ref, oa_ref, ob_ref, oc_ref, wbr_ref, wo_ref, nf_ref,
                      wfi_ref, wfo_ref, o_ref, act_ref, *, ff_chunk):
    mix = None
    for n, br in enumerate((oa_ref, ob_ref, oc_ref)):
        y = _mm(br[...], wbr_ref[n]) if n == 1 else _tn(br[...], wbr_ref[n])
        gt = jax.nn.sigmoid(gate_ref[:, n * D_MODEL:(n + 1) * D_MODEL].astype(F32))
        mix = gt * y if mix is None else mix + gt * y
    x = x_ref[...] + _mm(mix.astype(BF16), wo_ref[...])
    ms = jnp.mean(x * x, axis=-1, keepdims=True)
    h = (x * lax.rsqrt(ms + NORM_EPS) * nf_ref[...]).astype(BF16)
    for c in range(0, D_FF, ff_chunk):
        fg = _mm(h, wfi_ref[:, c:c + ff_chunk])
        fu = _mm(h, wfi_ref[:, D_FF + c:D_FF + c + ff_chunk])
        act_ref[:, c:c + ff_chunk] = (fg * jax.nn.sigmoid(fg) * fu).astype(BF16)
    o_ref[...] = x + _mm(act_ref[...], wfo_ref[...])


def _merge_ffn_call(x2, gate, oa, ob, oc, wbr, wo, nf, wfi, wfo, tm=512, ff_chunk=256):
    T = x2.shape[0]
    row = lambda w: pl.BlockSpec((tm, w), lambda i: (i, 0))
    col = pl.BlockSpec((BRANCH_W, tm), lambda i: (0, i))
    return pl.pallas_call(
        functools.partial(_merge_ffn_kernel, ff_chunk=ff_chunk),
        grid=(T // tm,),
        in_specs=[row(D_MODEL), row(3 * D_MODEL), col, row(BRANCH_W), col,
                  _resident(wbr.shape), _resident(wo.shape), _resident(nf.shape),
                  _resident(wfi.shape), _resident(wfo.shape)],
        out_specs=row(D_MODEL),
        out_shape=jax.ShapeDtypeStruct((T, D_MODEL), F32),
        scratch_shapes=[pltpu.VMEM((tm, D_FF), BF16)],
        compiler_params=pltpu.CompilerParams(dimension_semantics=("arbitrary",),
                                             vmem_limit_bytes=VMEM_LIMIT),
        name="merge_ffn",
    )(x2, gate, oa, ob, oc, wbr, wo, nf, wfi, wfo)


def _bucket_table():
    n = np.arange(2 * ATT_BLOCK)
    max_exact = REL_BUCKETS // 2
    nf = np.maximum(n, 1).astype(np.float32)
    large = max_exact + (np.log(nf / max_exact) / math.log(REL_MAX_DIST / max_exact)
                         * (REL_BUCKETS - max_exact)).astype(np.int32)
    large = np.minimum(large, REL_BUCKETS - 1)
    return np.where(n < max_exact, n, large).astype(np.int32)


def _bias_strips(rel_bias):
    period = 3 * ATT_BLOCK
    u = np.arange(period)
    dist = ATT_BLOCK + np.where(u < ATT_BLOCK, u, u - period)
    vec = jnp.where(jnp.asarray(dist >= 0)[None, :], rel_bias[_bucket_table()[np.maximum(dist, 0)]].T, NEG_INF)
    heads = vec.shape[0]
    skew = jnp.tile(vec, (1, 2 * ATT_BLOCK))[:, :2 * ATT_BLOCK * (period - 1)]
    return skew.reshape(heads, 2 * ATT_BLOCK, period - 1)[:, :, :ATT_BLOCK]


def _pad_cols(t, width):
    return jnp.pad(t, ((0, 0), (0, width - t.shape[1])))


def _pad_rows(t, height):
    return jnp.pad(t, ((0, height - t.shape[0]), (0, 0)))


def _hi_lo(w):
    hi = w.astype(BF16)
    return jnp.stack([hi, (w - hi.astype(F32)).astype(BF16)])


def _split_w_in(w):
    o = 0
    def take(n):
        nonlocal o
        s = w[:, o:o + n]
        o += n
        return s
    mq, mk, mv = take(512), take(512), take(512)
    r, wd, k, v, ad, gd = take(512), take(LORA_DECAY), take(512), take(512), take(LORA_AAA), take(LORA_GATE)
    dq, dc, diq, dik, diw = take(512), take(DSA_KV_RANK), take(256), take(IDX_DIM), take(IDX_HEADS)
    gate = take(3 * D_MODEL)
    w_tm = jnp.concatenate([gate, mk, r, k, v, _pad_cols(wd, 128), _pad_cols(ad, 128), _pad_cols(gd, 256),
                            dc, _pad_cols(dik, 128)], axis=1)
    w_cm = jnp.concatenate([mq, mv, dq, diq, diw], axis=1).T
    return w_tm.astype(BF16), w_cm.astype(BF16)


def _index_key_perm():
    p = np.zeros((2, 128, 128), np.float32)
    c = np.arange(IDX_DIM)
    p[0, c, c] = 1.0
    p[0, c, 2 * IDX_DIM + c] = 1.0
    p[1, c, IDX_DIM + c] = 1.0
    return jnp.asarray(p, BF16)


def _permute_mu(mu):
    r, wd, k, v, ad, gd = jnp.split(mu, np.cumsum([512, LORA_DECAY, 512, 512, LORA_AAA])[:].tolist())
    z = lambda n: jnp.zeros((n,), F32)
    return jnp.concatenate([r, k, v, wd, z(64), ad, z(64), gd, z(256 - LORA_GATE)])[None, :]


def kernel(x, w_in, norm_mix, norm_ffn, qk_norm, rel_bias, rwkv_mu, rwkv_w0, rwkv_w2, rwkv_a0, rwkv_a2, rwkv_g2, rwkv_kk, rwkv_ka, rwkv_rk, rwkv_ln_w, rwkv_ln_b, rwkv_v0, rwkv_va, rwkv_vb, dsa_kv_norm, dsa_kv_up, w_branch, w_o, w_ffn_in, w_ffn_out):
    B, S, D = x.shape
    depth = w_in.shape[0]
    T = B * S
    strips = _bias_strips(rel_bias * LOG2E)
    moba_bias, dsa_bias = strips[:N_HEADS], strips[N_HEADS:]
    head_of_lane = np.arange(BRANCH_W) // HEAD_DIM
    gmat = jnp.asarray(head_of_lane[:, None] == head_of_lane[None, :], BF16)
    tri = jnp.asarray(_chunk_cumsum_matrix(RWKV_TILE))
    perm = _index_key_perm()
    x2 = x.reshape(T, D)
    v_first = None
    for l in range(depth):
        w_tm, w_cm = _split_w_in(w_in[l])
        per_channel = lambda g: jnp.tile(g, N_HEADS)
        gq = jnp.stack([per_channel(qk_norm[l, 0]), per_channel(qk_norm[l, 2])]) * (LOG2E * HEAD_DIM ** -0.5)
        gq = jnp.broadcast_to(gq[:, :, None], (2, BRANCH_W, ATT_BLOCK))
        gk = jnp.stack([per_channel(qk_norm[l, 1]), per_channel(qk_norm[l, 3])])
        kv_up = dsa_kv_up[l].astype(BF16)
        (gate, mqt, mk, mvt, rw, dqt, dk, dvt, diqt, dik, diwt) = _proj_call(
            x2, norm_mix[l][None, :], w_tm, w_cm, dsa_kv_norm[l][None, :],
            kv_up[:, :BRANCH_W], kv_up[:, BRANCH_W:].T, gk, gq, gmat)
        o_moba = _moba_call(mqt, mk, mvt, moba_bias, B, S)
        o_dsa = _dsa_call(dqt, dk, dvt, diqt, dik, diwt, dsa_bias, perm, B, S)
        vec = jnp.stack([rwkv_w0[l], rwkv_a0[l], rwkv_kk[l], rwkv_ka[l]])
        vres = None
        if l > 0:
            vres = (v_first, rwkv_v0[l - 1][None, :], _pad_cols(rwkv_va[l - 1], 128).astype(BF16),
                    _pad_rows(rwkv_vb[l - 1], 128).astype(BF16))
        vec2 = jnp.stack([rwkv_ln_w[l], rwkv_ln_b[l], rwkv_rk[l].reshape(-1)])
        o_rwkv, v_new = _rwkv_call(
            rw, _permute_mu(rwkv_mu[l]), vec, _hi_lo(_pad_rows(rwkv_w2[l], 128)), _hi_lo(_pad_rows(rwkv_a2[l], 128)),
            _pad_rows(rwkv_g2[l], 256).astype(BF16), gmat, vec2, tri, B, S, vres)
        if l == 0:
            v_first = v_new
        x2 = _merge_ffn_call(x2, gate, o_moba, o_rwkv, o_dsa, w_branch[l].astype(BF16),
                             w_o[l].astype(BF16), norm_ffn[l][None, :],
                             w_ffn_in[l].astype(BF16), w_ffn_out[l].astype(BF16))
    return x2.reshape(B, S, D)
```

```python
import functools
import math

import numpy as np
import jax
import jax.numpy as jnp
from jax import lax
from jax.experimental import pallas as pl
from jax.experimental.pallas import tpu as pltpu

F32 = jnp.float32
BF16 = jnp.bfloat16
HIGHEST = lax.Precision.HIGHEST

D_MODEL = 1024
HEAD_DIM = 64
BRANCH_W = 512
N_HEADS = BRANCH_W // HEAD_DIM
MOBA_BLOCK = 256
MOBA_TOPK = 3
LORA_DECAY = 64
LORA_AAA = 64
LORA_GATE = 160
LORA_MV = 32
RWKV_GN_EPS = 64e-5
DSA_KV_RANK = 256
IDX_HEADS = 8
IDX_DIM = 32
DSA_TOPK_MAX = 256
REL_BUCKETS = 32
REL_MAX_DIST = 128
D_FF = 2816
NORM_EPS = 1e-6

ATT_BLOCK = 256
RWKV_CHUNK = 64
RWKV_TILE = 256
INT_MIN = -(2 ** 31)
NEG_INF = float("-inf")
LOG2E = math.log2(math.e)
VMEM_LIMIT = 52 * 1024 * 1024

TM_GATE, TM_MK, TM_RW, TM_DC, TM_DIK, N_TM = 0, 3072, 3584, 5632, 5888, 6016
CM_MQ, CM_MV, CM_DQ, CM_DIQ, CM_DIW, N_CM = 0, 512, 1024, 1536, 1792, 1800
RW_W = 2048


def _nt(a, b, precision=None):
    return lax.dot_general(a, b, (((1,), (1,)), ((), ())), precision=precision,
                           preferred_element_type=F32)


def _tn(a, b, precision=None):
    return lax.dot_general(a, b, (((0,), (0,)), ((), ())), precision=precision,
                           preferred_element_type=F32)


def _mm(a, b, precision=None):
    return jnp.dot(a, b, precision=precision, preferred_element_type=F32)


def _group_sum(t, gmat):
    hi = t.astype(BF16)
    lo = (t - hi.astype(F32)).astype(BF16)
    return _mm(hi, gmat) + _mm(lo, gmat)


def _head_rms(t, gain, gmat):
    ms = _mm((t * t).astype(BF16), gmat) * (1.0 / HEAD_DIM)
    return t * lax.rsqrt(ms + NORM_EPS) * gain


def _resident(shape):
    nd = len(shape)
    return pl.BlockSpec(shape, lambda *_: (0,) * nd, pipeline_mode=pl.Buffered(1))


def _head_rms_cm(t, gain):
    parts = []
    for h in range(N_HEADS):
        blk = t[h * HEAD_DIM:(h + 1) * HEAD_DIM, :]
        ms = jnp.mean(blk * blk, axis=0, keepdims=True)
        parts.append(blk * lax.rsqrt(ms + NORM_EPS))
    return jnp.concatenate(parts, axis=0) * gain


def _proj_kernel(x_ref, g_ref, wtm_ref, wcm_ref, kvn_ref, kvk_ref, kvvt_ref, qkn_ref, gq_ref, gmat_ref,
                 gate_o, mqt_o, mk_o, mvt_o, rw_o, dqt_o, dk_o, dvt_o, diqt_o, dik_o, diwt_o):
    x = x_ref[...]
    ms = jnp.mean(x * x, axis=-1, keepdims=True)
    h = (x * lax.rsqrt(ms + NORM_EPS) * g_ref[...]).astype(BF16)
    gmat = gmat_ref[...]

    def proj(off, width):
        return _mm(h, wtm_ref[:, off:off + width])

    def proj_t(off, width):
        return _nt(wcm_ref[off:off + width, :], h)

    gate_o[...] = proj(TM_GATE, 3 * D_MODEL).astype(BF16)
    mqt_o[...] = _head_rms_cm(proj_t(CM_MQ, BRANCH_W), gq_ref[0]).astype(BF16)
    mk_o[...] = _head_rms(proj(TM_MK, BRANCH_W), qkn_ref[0:1, :], gmat).astype(BF16)
    mvt_o[...] = proj_t(CM_MV, BRANCH_W).astype(BF16)
    rw_o[...] = proj(TM_RW, RW_W)
    dqt_o[...] = _head_rms_cm(proj_t(CM_DQ, BRANCH_W), gq_ref[1]).astype(BF16)
    dc = proj(TM_DC, DSA_KV_RANK)
    dcn = (dc * lax.rsqrt(jnp.mean(dc * dc, axis=-1, keepdims=True) + NORM_EPS) * kvn_ref[...]).astype(BF16)
    dk_o[...] = _head_rms(_mm(dcn, kvk_ref[...]), qkn_ref[1:2, :], gmat).astype(BF16)
    dvt_o[...] = _nt(kvvt_ref[...], dcn).astype(BF16)
    diqt_o[...] = proj_t(CM_DIQ, IDX_HEADS * IDX_DIM)
    dik_o[...] = proj(TM_DIK, 128)
    diwt_o[...] = proj_t(CM_DIW, IDX_HEADS)


def _proj_call(x2, g, w_tm, w_cm, kvn, kvk, kvvt, qkn, gq, gmat, tm=ATT_BLOCK):
    T = x2.shape[0]
    row = lambda w: pl.BlockSpec((tm, w), lambda i: (i, 0))
    col = lambda c: pl.BlockSpec((c, tm), lambda i: (0, i))
    tmaj = lambda w, d: (jax.ShapeDtypeStruct((T, w), d), row(w))
    cmaj = lambda c, d: (jax.ShapeDtypeStruct((c, T), d), col(c))
    outs = [tmaj(3 * D_MODEL, BF16), cmaj(BRANCH_W, BF16), tmaj(BRANCH_W, BF16), cmaj(BRANCH_W, BF16),
            tmaj(RW_W, F32), cmaj(BRANCH_W, BF16), tmaj(BRANCH_W, BF16), cmaj(BRANCH_W, BF16),
            cmaj(IDX_HEADS * IDX_DIM, F32), tmaj(128, F32), cmaj(IDX_HEADS, F32)]
    ins = [x2, g, w_tm, w_cm, kvn, kvk, kvvt, qkn, gq, gmat]
    return pl.pallas_call(
        _proj_kernel,
        grid=(T // tm,),
        in_specs=[row(D_MODEL)] + [_resident(t.shape) for t in ins[1:]],
        out_specs=[spec for _, spec in outs],
        out_shape=[shape for shape, _ in outs],
        compiler_params=pltpu.CompilerParams(dimension_semantics=("arbitrary",),
                                             vmem_limit_bytes=VMEM_LIMIT),
        name="proj",
    )(*ins)


ONES_ROWS = 16


def _masked_attention(i, qt_ref, k_ref, vt_ref, bias_ref, o_ref, mask_fn=None, query_mask_fn=None):
    Q = ATT_BLOCK
    H = range(N_HEADS)

    def blk(j):
        return pl.ds(pl.multiple_of(j * Q, Q), Q)

    prow = lax.broadcasted_iota(jnp.int32, (2 * HEAD_DIM, Q), 0)
    qm = []
    for p in range(N_HEADS // 2):
        pair = qt_ref[p * 2 * HEAD_DIM:(p + 1) * 2 * HEAD_DIM, :]
        zero = jnp.zeros_like(pair)
        qm += [jnp.where(prow < HEAD_DIM, pair, zero), jnp.where(prow >= HEAD_DIM, pair, zero)]
    ones = jnp.ones((ONES_ROWS, Q), BF16)

    def tile(j, carry, kind):
        kt = [k_ref[blk(j), p * 2 * HEAD_DIM:(p + 1) * 2 * HEAD_DIM] for p in range(N_HEADS // 2)]
        st = [_mm(kt[h // 2], qm[h]) for h in H]
        if kind == "prev":
            st = [st[h] + bias_ref[h, :Q, :] for h in H]
        elif kind == "own":
            st = [st[h] + bias_ref[h, Q:, :] for h in H]
        if mask_fn is not None:
            st = [mask_fn(h, j, st[h], kind) for h in H]
        tmax = [jnp.max(st[h], axis=0, keepdims=True) for h in H]
        attends = [None if query_mask_fn is None else query_mask_fn(h, j, kind) for h in H]
        tmax = [tmax[h] if attends[h] is None else jnp.where(attends[h], tmax[h], NEG_INF) for h in H]
        if kind == "far":
            far = [bias_ref[h, 0:1, Q - 1:Q] for h in H]
            tmax = [tmax[h] + far[h] for h in H]
        m_new = [jnp.maximum(carry[h][0], tmax[h]) for h in H]
        alpha = [jnp.exp2(carry[h][0] - m_new[h]) for h in H]
        sub = [m_new[h] - far[h] for h in H] if kind == "far" else m_new
        sub = [sub[h] if attends[h] is None else jnp.where(attends[h], sub[h], -NEG_INF) for h in H]
        p = [jnp.exp2(st[h] - sub[h]).astype(BF16) for h in H]
        vt = [jnp.concatenate([vt_ref[h * HEAD_DIM:(h + 1) * HEAD_DIM, blk(j)], ones], axis=0) for h in H]
        pv = [_mm(vt[h], p[h]) for h in H]
        return tuple((m_new[h], alpha[h] * carry[h][1] + pv[h]) for h in H)

    init = tuple((jnp.full((1, Q), -1e30, F32), jnp.zeros((HEAD_DIM + ONES_ROWS, Q), F32)) for _ in H)
    carry = lax.fori_loop(0, i - 1, lambda j, c: tile(j, c, "far"), init)
    carry = tile(jnp.maximum(i - 1, 0), carry, "prev")
    carry = tile(i, carry, "own")
    for h in H:
        acc = carry[h][1]
        o_ref[h * HEAD_DIM:(h + 1) * HEAD_DIM, :] = (acc[:HEAD_DIM] / acc[HEAD_DIM:HEAD_DIM + 1]).astype(o_ref.dtype)


def _moba_kernel(qt_ref, k_ref, vt_ref, bias_ref, o_ref, kmean_ref, sel_ref, *, nb):
    i = pl.program_id(1)
    Q = ATT_BLOCK

    @pl.when(i == 0)
    def _():
        means = [jnp.mean(k_ref[j * Q:(j + 1) * Q, :].astype(F32), axis=0, keepdims=True) for j in range(nb)]
        kmean = jnp.concatenate(means * N_HEADS, axis=0)
        shape = (N_HEADS * nb, BRANCH_W)
        row_head = lax.broadcasted_iota(jnp.int32, shape, 0) // nb
        lane_head = lax.broadcasted_iota(jnp.int32, shape, 1) // HEAD_DIM
        rest = jnp.where(row_head == lane_head, kmean, 0.0)
        for piece in range(3):
            part = rest.astype(BF16)
            kmean_ref[piece] = part
            rest = rest - part.astype(F32)

    qt = qt_ref[...]
    gate = _mm(kmean_ref[0], qt) + _mm(kmean_ref[1], qt) + _mm(kmean_ref[2], qt)
    gate = gate.reshape(N_HEADS, nb, Q)
    brow = lax.broadcasted_iota(jnp.int32, (N_HEADS, nb, Q), 1)
    n_sel = min(MOBA_TOPK, nb - 1)
    rank = jnp.zeros((N_HEADS, nb, Q), F32)
    for jp in range(nb):
        gj = gate[:, jp:jp + 1, :]
        ahead = (gj > gate) | ((gj == gate) & (jp < brow))
        rank = rank + jnp.where(ahead & (jp < i), 1.0, 0.0)
    sel_ref[...] = jnp.where((brow < i) & (rank < n_sel), 1.0, 0.0)

    def query_mask_fn(h, j, kind):
        return None if kind == "own" else sel_ref[h, pl.ds(j, 1), :] > 0.0

    _masked_attention(i, qt_ref, k_ref, vt_ref, bias_ref, o_ref, query_mask_fn=query_mask_fn)


def _moba_call(mqt, mk, mvt, bias_t, B, S):
    nb = S // ATT_BLOCK
    T = B * S
    qspec = pl.BlockSpec((BRANCH_W, ATT_BLOCK), lambda b, i: (0, b * nb + i))
    return pl.pallas_call(
        functools.partial(_moba_kernel, nb=nb),
        grid=(B, nb),
        in_specs=[qspec,
                  pl.BlockSpec((S, BRANCH_W), lambda b, i: (b, 0)),
                  pl.BlockSpec((BRANCH_W, S), lambda b, i: (0, b)),
                  _resident(bias_t.shape)],
        out_specs=qspec,
        out_shape=jax.ShapeDtypeStruct((BRANCH_W, T), BF16),
        scratch_shapes=[pltpu.VMEM((3, N_HEADS * nb, BRANCH_W), BF16),
                        pltpu.VMEM((N_HEADS, nb, ATT_BLOCK), F32)],
        compiler_params=pltpu.CompilerParams(dimension_semantics=("arbitrary", "arbitrary"),
                                             vmem_limit_bytes=VMEM_LIMIT),
        name="moba",
    )(mqt, mk, mvt, bias_t)


def _dsa_kernel(qt_ref, k_ref, vt_ref, qit_ref, wt_ref, kx_ref, bias_ref, perm_ref, o_ref,
                kx3_ref, keys_ref, khi_ref, klo_ref, mask_ref, cut_ref, *, n_keep, n_idx_bits, nq):
    i = pl.program_id(1)
    Q = ATT_BLOCK
    krow = lax.broadcasted_iota(jnp.int32, (Q, Q), 0)
    qcol = lax.broadcasted_iota(jnp.int32, (Q, Q), 1)

    def blk(j):
        return pl.ds(pl.multiple_of(j * Q, Q), Q)

    def split(t):
        hi = t.astype(BF16)
        return hi, (t - hi.astype(F32)).astype(BF16)

    @pl.when(i == 0)
    def _():
        for j in range(nq):
            hi, lo = split(kx_ref[j * Q:(j + 1) * Q, :])
            kx3_ref[j * Q:(j + 1) * Q, :] = (_mm(hi, perm_ref[0]) + _mm(lo, perm_ref[1])).astype(BF16)

    rhs = []
    for h in range(IDX_HEADS):
        hi, lo = split(qit_ref[h * IDX_DIM:(h + 1) * IDX_DIM, :])
        rhs.append(jnp.concatenate([hi, hi, lo, jnp.zeros_like(hi)], axis=0))

    def score_tile(j, own_tile):
        kt3 = kx3_ref[blk(j), :]
        acc = jnp.zeros((Q, Q), F32)
        for h in range(IDX_HEADS):
            acc = acc + jnp.maximum(_mm(kt3, rhs[h]), 0.0) * wt_ref[h:h + 1, :]
        acc = jnp.where(acc == 0.0, 0.0, acc)
        bits = pltpu.bitcast(acc, jnp.int32)
        key = jnp.where(bits < 0, bits ^ jnp.int32(0x7FFFFFFF), bits)
        if own_tile:
            key = jnp.where(krow <= qcol, key, jnp.int32(INT_MIN))
        keys_ref[blk(j), :] = key
        khi_ref[blk(j), :] = (key >> 16).astype(jnp.int16)
        klo_ref[blk(j), :] = ((key & jnp.int32(0xFFFF)) - 32768).astype(jnp.int16)
        return 0

    lax.fori_loop(0, i, lambda j, _: score_tile(j, False), 0)
    score_tile(i, True)

    def count16(ref, pred):
        rows = 16
        def body(j, acc):
            hit = jnp.where(pred(ref[blk(j), :]), jnp.int16(1), jnp.int16(0))
            parts = [hit[rows * r:rows * (r + 1), :] for r in range(Q // rows)]
            while len(parts) > 1:
                parts = [x + y for x, y in zip(parts[::2], parts[1::2])]
            return acc + parts[0]
        acc = lax.fori_loop(0, i + 1, body, jnp.zeros((rows, Q), jnp.int16))
        return jnp.sum(acc.astype(jnp.int32), axis=0, keepdims=True)

    def bisect16(ref, target):
        def body(it, v):
            cand = v + jnp.left_shift(jnp.int32(1), 15 - it)
            c16 = cand.astype(jnp.int16)
            return jnp.where(count16(ref, lambda x: x >= c16) >= target, cand, v)
        return lax.fori_loop(0, 16, body, jnp.full((1, Q), -32768, jnp.int32))

    def count(pred):
        def body(j, acc8):
            hit = jnp.where(pred(keys_ref[blk(j), :], j * Q), 1.0, 0.0)
            parts = [hit[8 * r:8 * r + 8, :] for r in range(Q // 8)]
            while len(parts) > 1:
                parts = [x + y for x, y in zip(parts[::2], parts[1::2])]
            return acc8 + parts[0]
        acc8 = lax.fori_loop(0, i + 1, body, jnp.zeros((8, Q), F32))
        return jnp.sum(acc8, axis=0, keepdims=True)

    thr_hi = bisect16(khi_ref, n_keep)
    hi16 = thr_hi.astype(jnp.int16)
    need_lo = n_keep - count16(khi_ref, lambda x: x > hi16)

    def lo_mask_body(j, _):
        klo_ref[blk(j), :] = jnp.where(khi_ref[blk(j), :] == hi16, klo_ref[blk(j), :], jnp.int16(-32768))
        return 0

    lax.fori_loop(0, i + 1, lo_mask_body, 0)
    thr = thr_hi * 65536 + (bisect16(klo_ref, need_lo) + 32768)

    n_gt = count(lambda kb, k0: kb > thr)
    n_ge = count(lambda kb, k0: kb >= thr)
    need = n_keep - n_gt
    excess = jnp.where((n_ge - n_gt > need) & (thr > jnp.int32(INT_MIN)), 1.0, 0.0)
    cut_ref[...] = jnp.full(cut_ref.shape, 2 ** 30, jnp.int32)

    @pl.when(jnp.max(excess) > 0.0)
    def _():
        def eq_index_body(j, _):
            idx = jnp.where(keys_ref[blk(j), :] == thr, j * Q + krow, jnp.int32(32767))
            klo_ref[blk(j), :] = idx.astype(jnp.int16)
            return 0

        lax.fori_loop(0, i + 1, eq_index_body, 0)

        def cut_body(it, cut):
            cand = cut + jnp.left_shift(jnp.int32(1), n_idx_bits - 1 - it)
            c16 = cand.astype(jnp.int16)
            cnt = count16(klo_ref, lambda x: x < c16)
            return jnp.where(cnt.astype(F32) <= need, cand, cut)
        cut_ref[0:1, :] = lax.fori_loop(0, n_idx_bits, cut_body, jnp.zeros((1, Q), jnp.int32))

    cut = jnp.where(thr > jnp.int32(INT_MIN), cut_ref[0:1, :], 0)

    def mask_body(j, _):
        kb = keys_ref[blk(j), :]
        tie = jnp.where((j * Q + krow) < cut, 0.0, NEG_INF)
        mask_ref[blk(j), :] = jnp.where(kb > thr, 0.0, jnp.where(kb == thr, tie, NEG_INF))
        return 0

    lax.fori_loop(0, i + 1, mask_body, 0)

    def mask_fn(h, j, st, kind):
        return st + mask_ref[blk(j), :]

    def query_mask_fn(h, j, kind):
        return jnp.broadcast_to(i >= 1, (1, Q)) if kind == "prev" else None

    _masked_attention(i, qt_ref, k_ref, vt_ref, bias_ref, o_ref, mask_fn, query_mask_fn)


def _dsa_call(dqt, dk, dvt, diqt, dik, diwt, bias_t, perm, B, S):
    nq = S // ATT_BLOCK
    T = B * S
    n_keep = min(DSA_TOPK_MAX, S // 4)
    assert S % ATT_BLOCK == 0 and S < 2 ** 14
    qcol = lambda c: pl.BlockSpec((c, ATT_BLOCK), lambda b, i: (0, b * nq + i))
    return pl.pallas_call(
        functools.partial(_dsa_kernel, n_keep=n_keep, n_idx_bits=int(math.log2(S)) + 1, nq=nq),
        grid=(B, nq),
        in_specs=[qcol(BRANCH_W),
                  pl.BlockSpec((S, BRANCH_W), lambda b, i: (b, 0)),
                  pl.BlockSpec((BRANCH_W, S), lambda b, i: (0, b)),
                  qcol(IDX_HEADS * IDX_DIM), qcol(IDX_HEADS),
                  pl.BlockSpec((S, 128), lambda b, i: (b, 0)),
                  _resident(bias_t.shape), _resident(perm.shape)],
        out_specs=qcol(BRANCH_W),
        out_shape=jax.ShapeDtypeStruct((BRANCH_W, T), BF16),
        scratch_shapes=[pltpu.VMEM((S, 128), BF16), pltpu.VMEM((S, ATT_BLOCK), jnp.int32),
                        pltpu.VMEM((S, ATT_BLOCK), jnp.int16), pltpu.VMEM((S, ATT_BLOCK), jnp.int16),
                        pltpu.VMEM((S, ATT_BLOCK), F32), pltpu.VMEM((8, ATT_BLOCK), jnp.int32)],
        compiler_params=pltpu.CompilerParams(dimension_semantics=("arbitrary", "arbitrary"),
                                             vmem_limit_bytes=VMEM_LIMIT),
        name="dsa",
    )(dqt, dk, dvt, diqt, diwt, dik, bias_t, perm)


def _dot3(x, w_ref):
    hi = x.astype(BF16)
    lo = (x - hi.astype(F32)).astype(BF16)
    return _mm(hi, w_ref[0]) + _mm(lo, w_ref[0]) + _mm(hi, w_ref[1])


def _rwkv_prologue(p, last, mu, vec_ref, w2_ref, a2_ref, g2_ref, gmat, vres_refs):
    rid = lax.broadcasted_iota(jnp.int32, p.shape, 0)
    shifted = jnp.where(rid == 0, last, pltpu.roll(p, 1, 0))
    p = p + (shifted - p) * mu
    r = p[:, 0:512]
    k = p[:, 512:1024]
    v = p[:, 1024:1536]
    wd = p[:, 1536:1664]
    ad = p[:, 1664:1792]
    gd = p[:, 1792:2048]
    w0, a0, k_k, k_a = vec_ref[0:1, :], vec_ref[1:2, :], vec_ref[2:3, :], vec_ref[3:4, :]
    z = -(w0 + _dot3(jnp.tanh(wd), w2_ref))
    softplus = jnp.maximum(z, 0.0) + jnp.log(1.0 + jnp.exp(-jnp.abs(z)))
    lw = -jnp.exp(-softplus - 0.5)
    a = jax.nn.sigmoid(a0 + _dot3(ad, a2_ref))
    g = _mm(jax.nn.sigmoid(gd).astype(BF16), g2_ref[...])
    if vres_refs is not None:
        vf_ref, v0_ref, va_ref, vb_ref = vres_refs
        low = _mm(v.astype(BF16), va_ref[...])
        mix = jax.nn.sigmoid(v0_ref[...] + _mm(low.astype(BF16), vb_ref[...]))
        v = v + (vf_ref[...] - v) * mix
    kk = k * k_k
    nrm = jnp.sqrt(_group_sum(kk * kk, gmat))
    kk = kk / jnp.maximum(nrm, 1e-12)
    return r, lw, k * (1.0 + (a - 1.0) * k_a), v, -kk, kk * a, g


def _rwkv_kernel(*refs, n_chunks, has_vres):
    (rw_ref, prev_ref, mu_ref, pre_vec_ref, w2_ref, a2_ref, g2_ref, gmat_ref, vec_ref, tri_ref), rest = refs[:10], refs[10:]
    if has_vres:
        vres_refs, (o_ref, state_ref, oraw_ref) = rest[:4], rest[4:]
    else:
        vres_refs, (o_ref, vfirst_ref, state_ref, oraw_ref) = None, rest
    C = RWKV_CHUNK
    N = HEAD_DIM
    bf = lambda t: t.astype(BF16)
    first_tile = pl.program_id(1) == 0

    @pl.when(first_tile)
    def _():
        state_ref[...] = jnp.zeros_like(state_ref)

    last = jnp.where(first_tile, 0.0, prev_ref[7:8, :])
    r, lw, k, v, a, b, g = _rwkv_prologue(rw_ref[...], last, mu_ref[...], pre_vec_ref, w2_ref, a2_ref,
                                          g2_ref, gmat_ref[...], vres_refs)
    if not has_vres:
        vfirst_ref[...] = v
    cum = _mm(tri_ref[...], lw, HIGHEST)
    e_neg = jnp.exp(-cum)
    at = a * jnp.exp(cum - lw)
    rt = r * jnp.exp(cum)
    bt = b * e_neg
    kt = k * e_neg
    bp, kp, p_end = [], [], []
    for c in range(n_chunks):
        rows = slice(c * C, (c + 1) * C)
        cum_last = cum[(c + 1) * C - 1:(c + 1) * C, :]
        e_end = jnp.exp(cum_last - cum[rows])
        bp.append(b[rows] * e_end)
        kp.append(k[rows] * e_end)
        p_end.append(jnp.exp(cum_last))

    rr = lax.broadcasted_iota(jnp.int32, (C, C), 0)
    cc = lax.broadcasted_iota(jnp.int32, (C, C), 1)
    strict = rr > cc
    incl = rr >= cc
    eye = jnp.where(rr == cc, 1.0, 0.0)
    pairs = [(c, h) for c in range(n_chunks) for h in range(N_HEADS)]
    sl = lambda t, c, h: t[c * C:(c + 1) * C, h * N:(h + 1) * N]
    hl = lambda t, h: t[:, h * N:(h + 1) * N]
    each = lambda f: {p: f(*p) for p in pairs}

    aall = each(lambda c, h: _nt(bf(jnp.concatenate([sl(at, c, h), sl(rt, c, h)], axis=0)),
                                 bf(jnp.concatenate([sl(bt, c, h), sl(kt, c, h)], axis=0))))
    vb = each(lambda c, h: bf(sl(v, c, h)))
    akv = each(lambda c, h: _mm(bf(jnp.where(strict, aall[c, h][:C, C:], 0.0)), vb[c, h]))
    o_loc = each(lambda c, h: _mm(bf(jnp.where(incl, aall[c, h][C:, C:], 0.0)), vb[c, h]))
    a_rb = each(lambda c, h: bf(jnp.where(incl, aall[c, h][C:, :C], 0.0)))
    x = each(lambda c, h: jnp.where(strict, aall[c, h][:C, :C], 0.0))
    tinv = each(lambda c, h: eye + x[c, h])
    for _ in range(int(math.log2(C)) - 1):
        x = each(lambda c, h: _mm(bf(x[c, h]), bf(x[c, h])))
        tinv = each(lambda c, h: tinv[c, h] + _mm(bf(tinv[c, h]), bf(x[c, h])))
    w_t = each(lambda c, h: _mm(bf(tinv[c, h]), bf(sl(at, c, h))))
    u_loc = each(lambda c, h: _mm(bf(tinv[c, h]), bf(akv[c, h])))

    heads = range(N_HEADS)
    state = [state_ref[h] for h in heads]
    for c in range(n_chunks):
        xs = [_nt(bf(jnp.concatenate([w_t[c, h], sl(rt, c, h)], axis=0)), bf(state[h])) for h in heads]
        u = [xs[h][:C] + u_loc[c, h] for h in heads]
        upd = [_tn(bf(jnp.concatenate([u[h], sl(v, c, h)], axis=0)),
                   bf(jnp.concatenate([hl(bp[c], h), hl(kp[c], h)], axis=0))) for h in heads]
        state = [state[h] * hl(p_end[c], h) + upd[h] for h in heads]
        o = [xs[h][C:] + _mm(a_rb[c, h], bf(u[h])) + o_loc[c, h] for h in heads]
        for h in heads:
            oraw_ref[c * C:(c + 1) * C, h * N:(h + 1) * N] = o[h]
    for h in heads:
        state_ref[h] = state[h]

    gmat = gmat_ref[...]
    o = oraw_ref[...]
    ln_w, ln_b, r_k = vec_ref[0:1, :], vec_ref[1:2, :], vec_ref[2:3, :]
    mean = _group_sum(o, gmat) * (1.0 / N)
    d = o - mean
    var = _group_sum(d * d, gmat) * (1.0 / N)
    o = d * lax.rsqrt(var + RWKV_GN_EPS) * ln_w + ln_b
    bonus = _group_sum(r * k * r_k, gmat)
    o_ref[...] = ((o + bonus * v) * g).astype(o_ref.dtype)


def _chunk_cumsum_matrix(tt):
    t = np.arange(tt)
    same_chunk = (t[:, None] // RWKV_CHUNK) == (t[None, :] // RWKV_CHUNK)
    return (same_chunk & (t[:, None] >= t[None, :])).astype(np.float32)


def _rwkv_call(rw, mu, pre_vec, w2, a2, g2, gmat, vec, tri, B, S, vres=None, tt=RWKV_TILE):
    T = B * S
    nt = S // tt
    row = lambda w: pl.BlockSpec((tt, w), lambda bi, t: (bi * nt + t, 0))
    prev = pl.BlockSpec((8, RW_W), lambda bi, t: (jnp.maximum((bi * nt + t) * (tt // 8) - 1, 0), 0))
    ins = [rw, rw, mu, pre_vec, w2, a2, g2, gmat, vec, tri]
    specs = [row(RW_W), prev] + [_resident(t.shape) for t in ins[2:]]
    out_specs, out_shape = [row(BRANCH_W)], [jax.ShapeDtypeStruct((T, BRANCH_W), BF16)]
    if vres is not None:
        vf, v0, va, vb = vres
        ins += [vf, v0, va, vb]
        specs += [row(BRANCH_W), _resident(v0.shape), _resident(va.shape), _resident(vb.shape)]
    else:
        out_specs.append(row(BRANCH_W))
        out_shape.append(jax.ShapeDtypeStruct((T, BRANCH_W), F32))
    outs = pl.pallas_call(
        functools.partial(_rwkv_kernel, n_chunks=tt // RWKV_CHUNK, has_vres=vres is not None),
        grid=(B, nt),
        in_specs=specs,
        out_specs=out_specs,
        out_shape=out_shape,
        scratch_shapes=[pltpu.VMEM((N_HEADS, HEAD_DIM, HEAD_DIM), F32),
                        pltpu.VMEM((tt, BRANCH_W), F32)],
        compiler_params=pltpu.CompilerParams(dimension_semantics=("arbitrary", "arbitrary"),
                                             vmem_limit_bytes=VMEM_LIMIT),
        name="rwkv",
    )(*ins)
    return (outs[0], None) if vres is not None else (outs[0], outs[1])


def _merge_ffn_kernel(x_ref, gate_ref, oa_ref, ob_ref, oc_ref, wbr_ref, wo_ref, nf_ref,
                      wfi_ref, wfo_ref, o_ref, act_ref, *, ff_chunk):
    mix = None
    for n, br in enumerate((oa_ref, ob_ref, oc_ref)):
        y = _mm(br[...], wbr_ref[n]) if n == 1 else _tn(br[...], wbr_ref[n])
        gt = jax.nn.sigmoid(gate_ref[:, n * D_MODEL:(n + 1) * D_MODEL].astype(F32))
        mix = gt * y if mix is None else mix + gt * y
    x = x_ref[...] + _mm(mix.astype(BF16), wo_ref[...])
    ms = jnp.mean(x * x, axis=-1, keepdims=True)
    h = (x * lax.rsqrt(ms + NORM_EPS) * nf_ref[...]).astype(BF16)
    for c in range(0, D_FF, ff_chunk):
        fg = _mm(h, wfi_ref[:, c:c + ff_chunk])
        fu = _mm(h, wfi_ref[:, D_FF + c:D_FF + c + ff_chunk])
        act_ref[:, c:c + ff_chunk] = (fg * jax.nn.sigmoid(fg) * fu).astype(BF16)
    o_ref[...] = x + _mm(act_ref[...], wfo_ref[...])


def _merge_ffn_call(x2, gate, oa, ob, oc, wbr, wo, nf, wfi, wfo, tm=512, ff_chunk=256):
    T = x2.shape[0]
    row = lambda w: pl.BlockSpec((tm, w), lambda i: (i, 0))
    col = pl.BlockSpec((BRANCH_W, tm), lambda i: (0, i))
    return pl.pallas_call(
        functools.partial(_merge_ffn_kernel, ff_chunk=ff_chunk),
        grid=(T // tm,),
        in_specs=[row(D_MODEL), row(3 * D_MODEL), col, row(BRANCH_W), col,
                  _resident(wbr.shape), _resident(wo.shape), _resident(nf.shape),
                  _resident(wfi.shape), _resident(wfo.shape)],
        out_specs=row(D_MODEL),
        out_shape=jax.ShapeDtypeStruct((T, D_MODEL), F32),
        scratch_shapes=[pltpu.VMEM((tm, D_FF), BF16)],
        compiler_params=pltpu.CompilerParams(dimension_semantics=("arbitrary",),
                                             vmem_limit_bytes=VMEM_LIMIT),
        name="merge_ffn",
    )(x2, gate, oa, ob, oc, wbr, wo, nf, wfi, wfo)


def _bucket_table():
    n = np.arange(2 * ATT_BLOCK)
    max_exact = REL_BUCKETS // 2
    nf = np.maximum(n, 1).astype(np.float32)
    large = max_exact + (np.log(nf / max_exact) / math.log(REL_MAX_DIST / max_exact)
                         * (REL_BUCKETS - max_exact)).astype(np.int32)
    large = np.minimum(large, REL_BUCKETS - 1)
    return np.where(n < max_exact, n, large).astype(np.int32)


def _bias_strips(rel_bias):
    period = 3 * ATT_BLOCK
    u = np.arange(period)
    dist = ATT_BLOCK + np.where(u < ATT_BLOCK, u, u - period)
    vec = jnp.where(jnp.asarray(dist >= 0)[None, :], rel_bias[_bucket_table()[np.maximum(dist, 0)]].T, NEG_INF)
    heads = vec.shape[0]
    skew = jnp.tile(vec, (1, 2 * ATT_BLOCK))[:, :2 * ATT_BLOCK * (period - 1)]
    return skew.reshape(heads, 2 * ATT_BLOCK, period - 1)[:, :, :ATT_BLOCK]


def _pad_cols(t, width):
    return jnp.pad(t, ((0, 0), (0, width - t.shape[1])))


def _pad_rows(t, height):
    return jnp.pad(t, ((0, height - t.shape[0]), (0, 0)))


def _hi_lo(w):
    hi = w.astype(BF16)
    return jnp.stack([hi, (w - hi.astype(F32)).astype(BF16)])


def _split_w_in(w):
    o = 0
    def take(n):
        nonlocal o
        s = w[:, o:o + n]
        o += n
        return s
    mq, mk, mv = take(512), take(512), take(512)
    r, wd, k, v, ad, gd = take(512), take(LORA_DECAY), take(512), take(512), take(LORA_AAA), take(LORA_GATE)
    dq, dc, diq, dik, diw = take(512), take(DSA_KV_RANK), take(256), take(IDX_DIM), take(IDX_HEADS)
    gate = take(3 * D_MODEL)
    w_tm = jnp.concatenate([gate, mk, r, k, v, _pad_cols(wd, 128), _pad_cols(ad, 128), _pad_cols(gd, 256),
                            dc, _pad_cols(dik, 128)], axis=1)
    w_cm = jnp.concatenate([mq, mv, dq, diq, diw], axis=1).T
    return w_tm.astype(BF16), w_cm.astype(BF16)


def _index_key_perm():
    p = np.zeros((2, 128, 128), np.float32)
    c = np.arange(IDX_DIM)
    p[0, c, c] = 1.0
    p[0, c, 2 * IDX_DIM + c] = 1.0
    p[1, c, IDX_DIM + c] = 1.0
    return jnp.asarray(p, BF16)


def _permute_mu(mu):
    r, wd, k, v, ad, gd = jnp.split(mu, np.cumsum([512, LORA_DECAY, 512, 512, LORA_AAA])[:].tolist())
    z = lambda n: jnp.zeros((n,), F32)
    return jnp.concatenate([r, k, v, wd, z(64), ad, z(64), gd, z(256 - LORA_GATE)])[None, :]


def kernel(x, w_in, norm_mix, norm_ffn, qk_norm, rel_bias, rwkv_mu, rwkv_w0, rwkv_w2, rwkv_a0, rwkv_a2, rwkv_g2, rwkv_kk, rwkv_ka, rwkv_rk, rwkv_ln_w, rwkv_ln_b, rwkv_v0, rwkv_va, rwkv_vb, dsa_kv_norm, dsa_kv_up, w_branch, w_o, w_ffn_in, w_ffn_out):
    B, S, D = x.shape
    depth = w_in.shape[0]
    T = B * S
    strips = _bias_strips(rel_bias * LOG2E)
    moba_bias, dsa_bias = strips[:N_HEADS], strips[N_HEADS:]
    head_of_lane = np.arange(BRANCH_W) // HEAD_DIM
    gmat = jnp.asarray(head_of_lane[:, None] == head_of_lane[None, :], BF16)
    tri = jnp.asarray(_chunk_cumsum_matrix(RWKV_TILE))
    perm = _index_key_perm()
    x2 = x.reshape(T, D)
    v_first = None
    for l in range(depth):
        w_tm, w_cm = _split_w_in(w_in[l])
        per_channel = lambda g: jnp.tile(g, N_HEADS)
        gq = jnp.stack([per_channel(qk_norm[l, 0]), per_channel(qk_norm[l, 2])]) * (LOG2E * HEAD_DIM ** -0.5)
        gq = jnp.broadcast_to(gq[:, :, None], (2, BRANCH_W, ATT_BLOCK))
        gk = jnp.stack([per_channel(qk_norm[l, 1]), per_channel(qk_norm[l, 3])])
        kv_up = dsa_kv_up[l].astype(BF16)
        (gate, mqt, mk, mvt, rw, dqt, dk, dvt, diqt, dik, diwt) = _proj_call(
            x2, norm_mix[l][None, :], w_tm, w_cm, dsa_kv_norm[l][None, :],
            kv_up[:, :BRANCH_W], kv_up[:, BRANCH_W:].T, gk, gq, gmat)
        o_moba = _moba_call(mqt, mk, mvt, moba_bias, B, S)
        o_dsa = _dsa_call(dqt, dk, dvt, diqt, dik, diwt, dsa_bias, perm, B, S)
        vec = jnp.stack([rwkv_w0[l], rwkv_a0[l], rwkv_kk[l], rwkv_ka[l]])
        vres = None
        if l > 0:
            vres = (v_first, rwkv_v0[l - 1][None, :], _pad_cols(rwkv_va[l - 1], 128).astype(BF16),
                    _pad_rows(rwkv_vb[l - 1], 128).astype(BF16))
        vec2 = jnp.stack([rwkv_ln_w[l], rwkv_ln_b[l], rwkv_rk[l].reshape(-1)])
        o_rwkv, v_new = _rwkv_call(
            rw, _permute_mu(rwkv_mu[l]), vec, _hi_lo(_pad_rows(rwkv_w2[l], 128)), _hi_lo(_pad_rows(rwkv_a2[l], 128)),
            _pad_rows(rwkv_g2[l], 256).astype(BF16), gmat, vec2, tri, B, S, vres)
        if l == 0:
            v_first = v_new
        x2 = _merge_ffn_call(x2, gate, o_moba, o_rwkv, o_dsa, w_branch[l].astype(BF16),
                             w_o[l].astype(BF16), norm_ffn[l][None, :],
                             w_ffn_in[l].astype(BF16), w_ffn_out[l].astype(BF16))
    return x2.reshape(B, S, D)
```

```python
import functools
import math

import numpy as np
import jax
import jax.numpy as jnp
from jax import lax
from jax.experimental import pallas as pl
from jax.experimental.pallas import tpu as pltpu

F32 = jnp.float32
BF16 = jnp.bfloat16
HIGHEST = lax.Precision.HIGHEST

D_MODEL = 1024
HEAD_DIM = 64
BRANCH_W = 512
N_HEADS = BRANCH_W // HEAD_DIM
MOBA_BLOCK = 256
MOBA_TOPK = 3
LORA_DECAY = 64
LORA_AAA = 64
LORA_GATE = 160
LORA_MV = 32
RWKV_GN_EPS = 64e-5
DSA_KV_RANK = 256
IDX_HEADS = 8
IDX_DIM = 32
DSA_TOPK_MAX = 256
REL_BUCKETS = 32
REL_MAX_DIST = 128
D_FF = 2816
NORM_EPS = 1e-6

ATT_BLOCK = 256
RWKV_CHUNK = 64
RWKV_TILE = 256
INT_MIN = -(2 ** 31)
NEG_INF = float("-inf")
LOG2E = math.log2(math.e)
VMEM_LIMIT = 52 * 1024 * 1024

TM_GATE, TM_MK, TM_RW, TM_DC, TM_DIK, N_TM = 0, 3072, 3584, 5632, 5888, 6016
CM_MQ, CM_MV, CM_DQ, CM_DIQ, CM_DIW, N_CM = 0, 512, 1024, 1536, 1792, 1800
RW_W = 2048


def _nt(a, b, precision=None):
    return lax.dot_general(a, b, (((1,), (1,)), ((), ())), precision=precision,
                           preferred_element_type=F32)


def _tn(a, b, precision=None):
    return lax.dot_general(a, b, (((0,), (0,)), ((), ())), precision=precision,
                           preferred_element_type=F32)


def _mm(a, b, precision=None):
    return jnp.dot(a, b, precision=precision, preferred_element_type=F32)


def _group_sum(t, gmat):
    hi = t.astype(BF16)
    lo = (t - hi.astype(F32)).astype(BF16)
    return _mm(hi, gmat) + _mm(lo, gmat)


def _head_rms(t, gain, gmat):
    ms = _mm((t * t).astype(BF16), gmat) * (1.0 / HEAD_DIM)
    return t * lax.rsqrt(ms + NORM_EPS) * gain


def _resident(shape):
    nd = len(shape)
    return pl.BlockSpec(shape, lambda *_: (0,) * nd, pipeline_mode=pl.Buffered(1))


def _head_rms_cm(t, gain):
    parts = []
    for h in range(N_HEADS):
        blk = t[h * HEAD_DIM:(h + 1) * HEAD_DIM, :]
        ms = jnp.mean(blk * blk, axis=0, keepdims=True)
        parts.append(blk * lax.rsqrt(ms + NORM_EPS))
    return jnp.concatenate(parts, axis=0) * gain


def _proj_kernel(x_ref, g_ref, wtm_ref, wcm_ref, kvn_ref, kvk_ref, kvvt_ref, qkn_ref, gq_ref, gmat_ref,
                 gate_o, mqt_o, mk_o, mvt_o, rw_o, dqt_o, dk_o, dvt_o, diqt_o, dik_o, diwt_o):
    x = x_ref[...]
    ms = jnp.mean(x * x, axis=-1, keepdims=True)
    h = (x * lax.rsqrt(ms + NORM_EPS) * g_ref[...]).astype(BF16)
    gmat = gmat_ref[...]

    def proj(off, width):
        return _mm(h, wtm_ref[:, off:off + width])

    def proj_t(off, width):
        return _nt(wcm_ref[off:off + width, :], h)

    gate_o[...] = proj(TM_GATE, 3 * D_MODEL).astype(BF16)
    mqt_o[...] = _head_rms_cm(proj_t(CM_MQ, BRANCH_W), gq_ref[0]).astype(BF16)
    mk_o[...] = _head_rms(proj(TM_MK, BRANCH_W), qkn_ref[0:1, :], gmat).astype(BF16)
    mvt_o[...] = proj_t(CM_MV, BRANCH_W).astype(BF16)
    rw_o[...] = proj(TM_RW, RW_W)
    dqt_o[...] = _head_rms_cm(proj_t(CM_DQ, BRANCH_W), gq_ref[1]).astype(BF16)
    dc = proj(TM_DC, DSA_KV_RANK)
    dcn = (dc * lax.rsqrt(jnp.mean(dc * dc, axis=-1, keepdims=True) + NORM_EPS) * kvn_ref[...]).astype(BF16)
    dk_o[...] = _head_rms(_mm(dcn, kvk_ref[...]), qkn_ref[1:2, :], gmat).astype(BF16)
    dvt_o[...] = _nt(kvvt_ref[...], dcn).astype(BF16)
    diqt_o[...] = proj_t(CM_DIQ, IDX_HEADS * IDX_DIM)
    dik_o[...] = proj(TM_DIK, 128)
    diwt_o[...] = proj_t(CM_DIW, IDX_HEADS)


def _proj_call(x2, g, w_tm, w_cm, kvn, kvk, kvvt, qkn, gq, gmat, tm=ATT_BLOCK):
    T = x2.shape[0]
    row = lambda w: pl.BlockSpec((tm, w), lambda i: (i, 0))
    col = lambda c: pl.BlockSpec((c, tm), lambda i: (0, i))
    tmaj = lambda w, d: (jax.ShapeDtypeStruct((T, w), d), row(w))
    cmaj = lambda c, d: (jax.ShapeDtypeStruct((c, T), d), col(c))
    outs = [tmaj(3 * D_MODEL, BF16), cmaj(BRANCH_W, BF16), tmaj(BRANCH_W, BF16), cmaj(BRANCH_W, BF16),
            tmaj(RW_W, F32), cmaj(BRANCH_W, BF16), tmaj(BRANCH_W, BF16), cmaj(BRANCH_W, BF16),
            cmaj(IDX_HEADS * IDX_DIM, F32), tmaj(128, F32), cmaj(IDX_HEADS, F32)]
    ins = [x2, g, w_tm, w_cm, kvn, kvk, kvvt, qkn, gq, gmat]
    return pl.pallas_call(
        _proj_kernel,
        grid=(T // tm,),
        in_specs=[row(D_MODEL)] + [_resident(t.shape) for t in ins[1:]],
        out_specs=[spec for _, spec in outs],
        out_shape=[shape for shape, _ in outs],
        compiler_params=pltpu.CompilerParams(dimension_semantics=("arbitrary",),
                                             vmem_limit_bytes=VMEM_LIMIT),
        name="proj",
    )(*ins)


ONES_ROWS = 16


def _masked_attention(i, qt_ref, k_ref, vt_ref, bias_ref, o_ref, mask_fn=None, query_mask_fn=None):
    Q = ATT_BLOCK
    H = range(N_HEADS)

    def blk(j):
        return pl.ds(pl.multiple_of(j * Q, Q), Q)

    prow = lax.broadcasted_iota(jnp.int32, (2 * HEAD_DIM, Q), 0)
    qm = []
    for p in range(N_HEADS // 2):
        pair = qt_ref[p * 2 * HEAD_DIM:(p + 1) * 2 * HEAD_DIM, :]
        zero = jnp.zeros_like(pair)
        qm += [jnp.where(prow < HEAD_DIM, pair, zero), jnp.where(prow >= HEAD_DIM, pair, zero)]
    ones = jnp.ones((ONES_ROWS, Q), BF16)

    def tile(j, carry, kind):
        kt = [k_ref[blk(j), p * 2 * HEAD_DIM:(p + 1) * 2 * HEAD_DIM] for p in range(N_HEADS // 2)]
        st = [_mm(kt[h // 2], qm[h]) for h in H]
        if kind == "prev":
            st = [st[h] + bias_ref[h, :Q, :] for h in H]
        elif kind == "own":
            st = [st[h] + bias_ref[h, Q:, :] for h in H]
        if mask_fn is not None:
            st = [mask_fn(h, j, st[h], kind) for h in H]
        tmax = [jnp.max(st[h], axis=0, keepdims=True) for h in H]
        attends = [None if query_mask_fn is None else query_mask_fn(h, j, kind) for h in H]
        tmax = [tmax[h] if attends[h] is None else jnp.where(attends[h], tmax[h], NEG_INF) for h in H]
        if kind == "far":
            far = [bias_ref[h, 0:1, Q - 1:Q] for h in H]
            tmax = [tmax[h] + far[h] for h in H]
        m_new = [jnp.maximum(carry[h][0], tmax[h]) for h in H]
        alpha = [jnp.exp2(carry[h][0] - m_new[h]) for h in H]
        sub = [m_new[h] - far[h] for h in H] if kind == "far" else m_new
        sub = [sub[h] if attends[h] is None else jnp.where(attends[h], sub[h], -NEG_INF) for h in H]
        p = [jnp.exp2(st[h] - sub[h]).astype(BF16) for h in H]
        vt = [jnp.concatenate([vt_ref[h * HEAD_DIM:(h + 1) * HEAD_DIM, blk(j)], ones], axis=0) for h in H]
        pv = [_mm(vt[h], p[h]) for h in H]
        return tuple((m_new[h], alpha[h] * carry[h][1] + pv[h]) for h in H)

    init = tuple((jnp.full((1, Q), -1e30, F32), jnp.zeros((HEAD_DIM + ONES_ROWS, Q), F32)) for _ in H)
    carry = lax.fori_loop(0, i - 1, lambda j, c: tile(j, c, "far"), init)
    carry = tile(jnp.maximum(i - 1, 0), carry, "prev")
    carry = tile(i, carry, "own")
    for h in H:
        acc = carry[h][1]
        o_ref[h * HEAD_DIM:(h + 1) * HEAD_DIM, :] = (acc[:HEAD_DIM] / acc[HEAD_DIM:HEAD_DIM + 1]).astype(o_ref.dtype)


def _moba_kernel(qt_ref, k_ref, vt_ref, bias_ref, o_ref, kmean_ref, sel_ref, *, nb):
    i = pl.program_id(1)
    Q = ATT_BLOCK

    @pl.when(i == 0)
    def _():
        means = [jnp.mean(k_ref[j * Q:(j + 1) * Q, :].astype(F32), axis=0, keepdims=True) for j in range(nb)]
        kmean = jnp.concatenate(means * N_HEADS, axis=0)
        shape = (N_HEADS * nb, BRANCH_W)
        row_head = lax.broadcasted_iota(jnp.int32, shape, 0) // nb
        lane_head = lax.broadcasted_iota(jnp.int32, shape, 1) // HEAD_DIM
        rest = jnp.where(row_head == lane_head, kmean, 0.0)
        for piece in range(3):
            part = rest.astype(BF16)
            kmean_ref[piece] = part
            rest = rest - part.astype(F32)

    qt = qt_ref[...]
    gate = _mm(kmean_ref[0], qt) + _mm(kmean_ref[1], qt) + _mm(kmean_ref[2], qt)
    gate = gate.reshape(N_HEADS, nb, Q)
    brow = lax.broadcasted_iota(jnp.int32, (N_HEADS, nb, Q), 1)
    n_sel = min(MOBA_TOPK, nb - 1)
    rank = jnp.zeros((N_HEADS, nb, Q), F32)
    for jp in range(nb):
        gj = gate[:, jp:jp + 1, :]
        ahead = (gj > gate) | ((gj == gate) & (jp < brow))
        rank = rank + jnp.where(ahead & (jp < i), 1.0, 0.0)
    sel_ref[...] = jnp.where((brow < i) & (rank < n_sel), 1.0, 0.0)

    def query_mask_fn(h, j, kind):
        return None if kind == "own" else sel_ref[h, pl.ds(j, 1), :] > 0.0

    _masked_attention(i, qt_ref, k_ref, vt_ref, bias_ref, o_ref, query_mask_fn=query_mask_fn)


def _moba_call(mqt, mk, mvt, bias_t, B, S):
    nb = S // ATT_BLOCK
    T = B * S
    qspec = pl.BlockSpec((BRANCH_W, ATT_BLOCK), lambda b, i: (0, b * nb + i))
    return pl.pallas_call(
        functools.partial(_moba_kernel, nb=nb),
        grid=(B, nb),
        in_specs=[qspec,
                  pl.BlockSpec((S, BRANCH_W), lambda b, i: (b, 0)),
                  pl.BlockSpec((BRANCH_W, S), lambda b, i: (0, b)),
                  _resident(bias_t.shape)],
        out_specs=qspec,
        out_shape=jax.ShapeDtypeStruct((BRANCH_W, T), BF16),
        scratch_shapes=[pltpu.VMEM((3, N_HEADS * nb, BRANCH_W), BF16),
                        pltpu.VMEM((N_HEADS, nb, ATT_BLOCK), F32)],
        compiler_params=pltpu.CompilerParams(dimension_semantics=("arbitrary", "arbitrary"),
                                             vmem_limit_bytes=VMEM_LIMIT),
        name="moba",
    )(mqt, mk, mvt, bias_t)


def _dsa_kernel(qt_ref, k_ref, vt_ref, qit_ref, wt_ref, kx_ref, bias_ref, perm_ref, o_ref,
                kx3_ref, keys_ref, khi_ref, klo_ref, mask_ref, cut_ref, *, n_keep, n_idx_bits, nq):
    i = pl.program_id(1)
    Q = ATT_BLOCK
    krow = lax.broadcasted_iota(jnp.int32, (Q, Q), 0)
    qcol = lax.broadcasted_iota(jnp.int32, (Q, Q), 1)

    def blk(j):
        return pl.ds(pl.multiple_of(j * Q, Q), Q)

    def split(t):
        hi = t.astype(BF16)
        return hi, (t - hi.astype(F32)).astype(BF16)

    @pl.when(i == 0)
    def _():
        for j in range(nq):
            hi, lo = split(kx_ref[j * Q:(j + 1) * Q, :])
            kx3_ref[j * Q:(j + 1) * Q, :] = (_mm(hi, perm_ref[0]) + _mm(lo, perm_ref[1])).astype(BF16)

    rhs = []
    for h in range(IDX_HEADS):
        hi, lo = split(qit_ref[h * IDX_DIM:(h + 1) * IDX_DIM, :])
        rhs.append(jnp.concatenate([hi, hi, lo, jnp.zeros_like(hi)], axis=0))

    def score_tile(j, own_tile):
        kt3 = kx3_ref[blk(j), :]
        acc = jnp.zeros((Q, Q), F32)
        for h in range(IDX_HEADS):
            acc = acc + jnp.maximum(_mm(kt3, rhs[h]), 0.0) * wt_ref[h:h + 1, :]
        acc = jnp.where(acc == 0.0, 0.0, acc)
        bits = pltpu.bitcast(acc, jnp.int32)
        key = jnp.where(bits < 0, bits ^ jnp.int32(0x7FFFFFFF), bits)
        if own_tile:
            key = jnp.where(krow <= qcol, key, jnp.int32(INT_MIN))
        keys_ref[blk(j), :] = key
        khi_ref[blk(j), :] = (key >> 16).astype(jnp.int16)
        klo_ref[blk(j), :] = ((key & jnp.int32(0xFFFF)) - 32768).astype(jnp.int16)
        return 0

    lax.fori_loop(0, i, lambda j, _: score_tile(j, False), 0)
    score_tile(i, True)

    def count16(ref, pred):
        rows = 16
        def body(j, acc):
            hit = jnp.where(pred(ref[blk(j), :]), jnp.int16(1), jnp.int16(0))
            parts = [hit[rows * r:rows * (r + 1), :] for r in range(Q // rows)]
            while len(parts) > 1:
                parts = [x + y for x, y in zip(parts[::2], parts[1::2])]
            return acc + parts[0]
        acc = lax.fori_loop(0, i + 1, body, jnp.zeros((rows, Q), jnp.int16))
        return jnp.sum(acc.astype(jnp.int32), axis=0, keepdims=True)

    def bisect16(ref, target):
        def body(it, v):
            cand = v + jnp.left_shift(jnp.int32(1), 15 - it)
            c16 = cand.astype(jnp.int16)
            return jnp.where(count16(ref, lambda x: x >= c16) >= target, cand, v)
        return lax.fori_loop(0, 16, body, jnp.full((1, Q), -32768, jnp.int32))

    def count(pred):
        def body(j, acc8):
            hit = jnp.where(pred(keys_ref[blk(j), :], j * Q), 1.0, 0.0)
            parts = [hit[8 * r:8 * r + 8, :] for r in range(Q // 8)]
            while len(parts) > 1:
                parts = [x + y for x, y in zip(parts[::2], parts[1::2])]
            return acc8 + parts[0]
        acc8 = lax.fori_loop(0, i + 1, body, jnp.zeros((8, Q), F32))
        return jnp.sum(acc8, axis=0, keepdims=True)

    thr_hi = bisect16(khi_ref, n_keep)
    hi16 = thr_hi.astype(jnp.int16)
    need_lo = n_keep - count16(khi_ref, lambda x: x > hi16)

    def lo_mask_body(j, _):
        klo_ref[blk(j), :] = jnp.where(khi_ref[blk(j), :] == hi16, klo_ref[blk(j), :], jnp.int16(-32768))
        return 0

    lax.fori_loop(0, i + 1, lo_mask_body, 0)
    thr = thr_hi * 65536 + (bisect16(klo_ref, need_lo) + 32768)

    n_gt = count(lambda kb, k0: kb > thr)
    n_ge = count(lambda kb, k0: kb >= thr)
    need = n_keep - n_gt
    excess = jnp.where((n_ge - n_gt > need) & (thr > jnp.int32(INT_MIN)), 1.0, 0.0)
    cut_ref[...] = jnp.full(cut_ref.shape, 2 ** 30, jnp.int32)

    @pl.when(jnp.max(excess) > 0.0)
    def _():
        def eq_index_body(j, _):
            idx = jnp.where(keys_ref[blk(j), :] == thr, j * Q + krow, jnp.int32(32767))
            klo_ref[blk(j), :] = idx.astype(jnp.int16)
            return 0

        lax.fori_loop(0, i + 1, eq_index_body, 0)

        def cut_body(it, cut):
            cand = cut + jnp.left_shift(jnp.int32(1), n_idx_bits - 1 - it)
            c16 = cand.astype(jnp.int16)
            cnt = count16(klo_ref, lambda x: x < c16)
            return jnp.where(cnt.astype(F32) <= need, cand, cut)
        cut_ref[0:1, :] = lax.fori_loop(0, n_idx_bits, cut_body, jnp.zeros((1, Q), jnp.int32))

    cut = jnp.where(thr > jnp.int32(INT_MIN), cut_ref[0:1, :], 0)

    def mask_body(j, _):
        kb = keys_ref[blk(j), :]
        tie = jnp.where((j * Q + krow) < cut, 0.0, NEG_INF)
        mask_ref[blk(j), :] = jnp.where(kb > thr, 0.0, jnp.where(kb == thr, tie, NEG_INF))
        return 0

    lax.fori_loop(0, i + 1, mask_body, 0)

    def mask_fn(h, j, st, kind):
        return st + mask_ref[blk(j), :]

    def query_mask_fn(h, j, kind):
        return jnp.broadcast_to(i >= 1, (1, Q)) if kind == "prev" else None

    _masked_attention(i, qt_ref, k_ref, vt_ref, bias_ref, o_ref, mask_fn, query_mask_fn)


def _dsa_call(dqt, dk, dvt, diqt, dik, diwt, bias_t, perm, B, S):
    nq = S // ATT_BLOCK
    T = B * S
    n_keep = min(DSA_TOPK_MAX, S // 4)
    assert S % ATT_BLOCK == 0 and S < 2 ** 14
    qcol = lambda c: pl.BlockSpec((c, ATT_BLOCK), lambda b, i: (0, b * nq + i))
    return pl.pallas_call(
        functools.partial(_dsa_kernel, n_keep=n_keep, n_idx_bits=int(math.log2(S)) + 1, nq=nq),
        grid=(B, nq),
        in_specs=[qcol(BRANCH_W),
                  pl.BlockSpec((S, BRANCH_W), lambda b, i: (b, 0)),
                  pl.BlockSpec((BRANCH_W, S), lambda b, i: (0, b)),
                  qcol(IDX_HEADS * IDX_DIM), qcol(IDX_HEADS),
                  pl.BlockSpec((S, 128), lambda b, i: (b, 0)),
                  _resident(bias_t.shape), _resident(perm.shape)],
        out_specs=qcol(BRANCH_W),
        out_shape=jax.ShapeDtypeStruct((BRANCH_W, T), BF16),
        scratch_shapes=[pltpu.VMEM((S, 128), BF16), pltpu.VMEM((S, ATT_BLOCK), jnp.int32),
                        pltpu.VMEM((S, ATT_BLOCK), jnp.int16), pltpu.VMEM((S, ATT_BLOCK), jnp.int16),
                        pltpu.VMEM((S, ATT_BLOCK), F32), pltpu.VMEM((8, ATT_BLOCK), jnp.int32)],
        compiler_params=pltpu.CompilerParams(dimension_semantics=("arbitrary", "arbitrary"),
                                             vmem_limit_bytes=VMEM_LIMIT),
        name="dsa",
    )(dqt, dk, dvt, diqt, diwt, dik, bias_t, perm)


def _dot3(x, w_ref):
    hi = x.astype(BF16)
    lo = (x - hi.astype(F32)).astype(BF16)
    return _mm(hi, w_ref[0]) + _mm(lo, w_ref[0]) + _mm(hi, w_ref[1])


def _rwkv_prologue(p, last, mu, vec_ref, w2_ref, a2_ref, g2_ref, gmat, vres_refs):
    rid = lax.broadcasted_iota(jnp.int32, p.shape, 0)
    shifted = jnp.where(rid == 0, last, pltpu.roll(p, 1, 0))
    p = p + (shifted - p) * mu
    r = p[:, 0:512]
    k = p[:, 512:1024]
    v = p[:, 1024:1536]
    wd = p[:, 1536:1664]
    ad = p[:, 1664:1792]
    gd = p[:, 1792:2048]
    w0, a0, k_k, k_a = vec_ref[0:1, :], vec_ref[1:2, :], vec_ref[2:3, :], vec_ref[3:4, :]
    z = -(w0 + _dot3(jnp.tanh(wd), w2_ref))
    softplus = jnp.maximum(z, 0.0) + jnp.log(1.0 + jnp.exp(-jnp.abs(z)))
    lw = -jnp.exp(-softplus - 0.5)
    a = jax.nn.sigmoid(a0 + _dot3(ad, a2_ref))
    g = _mm(jax.nn.sigmoid(gd).astype(BF16), g2_ref[...])
    if vres_refs is not None:
        vf_ref, v0_ref, va_ref, vb_ref = vres_refs
        low = _mm(v.astype(BF16), va_ref[...])
        mix = jax.nn.sigmoid(v0_ref[...] + _mm(low.astype(BF16), vb_ref[...]))
        v = v + (vf_ref[...] - v) * mix
    kk = k * k_k
    nrm = jnp.sqrt(_group_sum(kk * kk, gmat))
    kk = kk / jnp.maximum(nrm, 1e-12)
    return r, lw, k * (1.0 + (a - 1.0) * k_a), v, -kk, kk * a, g


def _rwkv_kernel(*refs, n_chunks, has_vres):
    (rw_ref, prev_ref, mu_ref, pre_vec_ref, w2_ref, a2_ref, g2_ref, gmat_ref, vec_ref, tri_ref), rest = refs[:10], refs[10:]
    if has_vres:
        vres_refs, (o_ref, state_ref, oraw_ref) = rest[:4], rest[4:]
    else:
        vres_refs, (o_ref, vfirst_ref, state_ref, oraw_ref) = None, rest
    C = RWKV_CHUNK
    N = HEAD_DIM
    bf = lambda t: t.astype(BF16)
    first_tile = pl.program_id(1) == 0

    @pl.when(first_tile)
    def _():
        state_ref[...] = jnp.zeros_like(state_ref)

    last = jnp.where(first_tile, 0.0, prev_ref[7:8, :])
    r, lw, k, v, a, b, g = _rwkv_prologue(rw_ref[...], last, mu_ref[...], pre_vec_ref, w2_ref, a2_ref,
                                          g2_ref, gmat_ref[...], vres_refs)
    if not has_vres:
        vfirst_ref[...] = v
    cum = _mm(tri_ref[...], lw, HIGHEST)
    e_neg = jnp.exp(-cum)
    at = a * jnp.exp(cum - lw)
    rt = r * jnp.exp(cum)
    bt = b * e_neg
    kt = k * e_neg
    bp, kp, p_end = [], [], []
    for c in range(n_chunks):
        rows = slice(c * C, (c + 1) * C)
        cum_last = cum[(c + 1) * C - 1:(c + 1) * C, :]
        e_end = jnp.exp(cum_last - cum[rows])
        bp.append(b[rows] * e_end)
        kp.append(k[rows] * e_end)
        p_end.append(jnp.exp(cum_last))

    rr = lax.broadcasted_iota(jnp.int32, (C, C), 0)
    cc = lax.broadcasted_iota(jnp.int32, (C, C), 1)
    strict = rr > cc
    incl = rr >= cc
    eye = jnp.where(rr == cc, 1.0, 0.0)
    pairs = [(c, h) for c in range(n_chunks) for h in range(N_HEADS)]
    sl = lambda t, c, h: t[c * C:(c + 1) * C, h * N:(h + 1) * N]
    hl = lambda t, h: t[:, h * N:(h + 1) * N]
    each = lambda f: {p: f(*p) for p in pairs}

    aall = each(lambda c, h: _nt(bf(jnp.concatenate([sl(at, c, h), sl(rt, c, h)], axis=0)),
                                 bf(jnp.concatenate([sl(bt, c, h), sl(kt, c, h)], axis=0))))
    vb = each(lambda c, h: bf(sl(v, c, h)))
    akv = each(lambda c, h: _mm(bf(jnp.where(strict, aall[c, h][:C, C:], 0.0)), vb[c, h]))
    o_loc = each(lambda c, h: _mm(bf(jnp.where(incl, aall[c, h][C:, C:], 0.0)), vb[c, h]))
    a_rb = each(lambda c, h: bf(jnp.where(incl, aall[c, h][C:, :C], 0.0)))
    x = each(lambda c, h: jnp.where(strict, aall[c, h][:C, :C], 0.0))
    tinv = each(lambda c, h: eye + x[c, h])
    for _ in range(int(math.log2(C)) - 1):
        x = each(lambda c, h: _mm(bf(x[c, h]), bf(x[c, h])))
        tinv = each(lambda c, h: tinv[c, h] + _mm(bf(tinv[c, h]), bf(x[c, h])))
    w_t = each(lambda c, h: _mm(bf(tinv[c, h]), bf(sl(at, c, h))))
    u_loc = each(lambda c, h: _mm(bf(tinv[c, h]), bf(akv[c, h])))

    heads = range(N_HEADS)
    state = [state_ref[h] for h in heads]
    for c in range(n_chunks):
        xs = [_nt(bf(jnp.concatenate([w_t[c, h], sl(rt, c, h)], axis=0)), bf(state[h])) for h in heads]
        u = [xs[h][:C] + u_loc[c, h] for h in heads]
        upd = [_tn(bf(jnp.concatenate([u[h], sl(v, c, h)], axis=0)),
                   bf(jnp.concatenate([hl(bp[c], h), hl(kp[c], h)], axis=0))) for h in heads]
        state = [state[h] * hl(p_end[c], h) + upd[h] for h in heads]
        o = [xs[h][C:] + _mm(a_rb[c, h], bf(u[h])) + o_loc[c, h] for h in heads]
        for h in heads:
            oraw_ref[c * C:(c + 1) * C, h * N:(h + 1) * N] = o[h]
    for h in heads:
        state_ref[h] = state[h]

    gmat = gmat_ref[...]
    o = oraw_ref[...]
    ln_w, ln_b, r_k = vec_ref[0:1, :], vec_ref[1:2, :], vec_ref[2:3, :]
    mean = _group_sum(o, gmat) * (1.0 / N)
    d = o - mean
    var = _group_sum(d * d, gmat) * (1.0 / N)
    o = d * lax.rsqrt(var + RWKV_GN_EPS) * ln_w + ln_b
    bonus = _group_sum(r * k * r_k, gmat)
    o_ref[...] = ((o + bonus * v) * g).astype(o_ref.dtype)


def _chunk_cumsum_matrix(tt):
    t = np.arange(tt)
    same_chunk = (t[:, None] // RWKV_CHUNK) == (t[None, :] // RWKV_CHUNK)
    return (same_chunk & (t[:, None] >= t[None, :])).astype(np.float32)


def _rwkv_call(rw, mu, pre_vec, w2, a2, g2, gmat, vec, tri, B, S, vres=None, tt=RWKV_TILE):
    T = B * S
    nt = S // tt
    row = lambda w: pl.BlockSpec((tt, w), lambda bi, t: (bi * nt + t, 0))
    prev = pl.BlockSpec((8, RW_W), lambda bi, t: (jnp.maximum((bi * nt + t) * (tt // 8) - 1, 0), 0))
    ins = [rw, rw, mu, pre_vec, w2, a2, g2, gmat, vec, tri]
    specs = [row(RW_W), prev] + [_resident(t.shape) for t in ins[2:]]
    out_specs, out_shape = [row(BRANCH_W)], [jax.ShapeDtypeStruct((T, BRANCH_W), BF16)]
    if vres is not None:
        vf, v0, va, vb = vres
        ins += [vf, v0, va, vb]
        specs += [row(BRANCH_W), _resident(v0.shape), _resident(va.shape), _resident(vb.shape)]
    else:
        out_specs.append(row(BRANCH_W))
        out_shape.append(jax.ShapeDtypeStruct((T, BRANCH_W), F32))
    outs = pl.pallas_call(
        functools.partial(_rwkv_kernel, n_chunks=tt // RWKV_CHUNK, has_vres=vres is not None),
        grid=(B, nt),
        in_specs=specs,
        out_specs=out_specs,
        out_shape=out_shape,
        scratch_shapes=[pltpu.VMEM((N_HEADS, HEAD_DIM, HEAD_DIM), F32),
                        pltpu.VMEM((tt, BRANCH_W), F32)],
        compiler_params=pltpu.CompilerParams(dimension_semantics=("arbitrary", "arbitrary"),
                                             vmem_limit_bytes=VMEM_LIMIT),
        name="rwkv",
    )(*ins)
    return (outs[0], None) if vres is not None else (outs[0], outs[1])


def _merge_ffn_kernel(x_ref, gate_ref, oa_ref, ob_ref, oc_ref, wbr_ref, wo_ref, nf_ref,
                      wfi_ref, wfo_ref, o_ref, act_ref, *, ff_chunk):
    mix = None
    for n, br in enumerate((oa_ref, ob_ref, oc_ref)):
        y = _mm(br[...], wbr_ref[n]) if n == 1 else _tn(br[...], wbr_ref[n])
        gt = jax.nn.sigmoid(gate_ref[:, n * D_MODEL:(n + 1) * D_MODEL].astype(F32))
        mix = gt * y if mix is None else mix + gt * y
    x = x_ref[...] + _mm(mix.astype(BF16), wo_ref[...])
    ms = jnp.mean(x * x, axis=-1, keepdims=True)
    h = (x * lax.rsqrt(ms + NORM_EPS) * nf_ref[...]).astype(BF16)
    for c in range(0, D_FF, ff_chunk):
        fg = _mm(h, wfi_ref[:, c:c + ff_chunk])
        fu = _mm(h, wfi_ref[:, D_FF + c:D_FF + c + ff_chunk])
        act_ref[:, c:c + ff_chunk] = (fg * jax.nn.sigmoid(fg) * fu).astype(BF16)
    o_ref[...] = x + _mm(act_ref[...], wfo_ref[...])


def _merge_ffn_call(x2, gate, oa, ob, oc, wbr, wo, nf, wfi, wfo, tm=512, ff_chunk=256):
    T = x2.shape[0]
    row = lambda w: pl.BlockSpec((tm, w), lambda i: (i, 0))
    col = pl.BlockSpec((BRANCH_W, tm), lambda i: (0, i))
    return pl.pallas_call(
        functools.partial(_merge_ffn_kernel, ff_chunk=ff_chunk),
        grid=(T // tm,),
        in_specs=[row(D_MODEL), row(3 * D_MODEL), col, row(BRANCH_W), col,
                  _resident(wbr.shape), _resident(wo.shape), _resident(nf.shape),
                  _resident(wfi.shape), _resident(wfo.shape)],
        out_specs=row(D_MODEL),
        out_shape=jax.ShapeDtypeStruct((T, D_MODEL), F32),
        scratch_shapes=[pltpu.VMEM((tm, D_FF), BF16)],
        compiler_params=pltpu.CompilerParams(dimension_semantics=("arbitrary",),
                                             vmem_limit_bytes=VMEM_LIMIT),
        name="merge_ffn",
    )(x2, gate, oa, ob, oc, wbr, wo, nf, wfi, wfo)


def _bucket_table():
    n = np.arange(2 * ATT_BLOCK)
    max_exact = REL_BUCKETS // 2
    nf = np.maximum(n, 1).astype(np.float32)
    large = max_exact + (np.log(nf / max_exact) / math.log(REL_MAX_DIST / max_exact)
                         * (REL_BUCKETS - max_exact)).astype(np.int32)
    large = np.minimum(large, REL_BUCKETS - 1)
    return np.where(n < max_exact, n, large).astype(np.int32)


def _strip_kernel(vec_ref, o_ref):
    rows = jnp.broadcast_to(vec_ref[0], (2 * ATT_BLOCK, vec_ref.shape[2]))
    o_ref[0] = pltpu.roll(rows, 0, 1, stride=1, stride_axis=0)[:, :ATT_BLOCK]


def _bias_strips(rel_bias):
    period = 3 * ATT_BLOCK
    u = np.arange(period)
    dist = ATT_BLOCK + np.where(u < ATT_BLOCK, u, u - period)
    vec = jnp.where(jnp.asarray(dist >= 0)[None, :], rel_bias[_bucket_table()[np.maximum(dist, 0)]].T, NEG_INF)
    heads = vec.shape[0]
    return pl.pallas_call(
        _strip_kernel,
        grid=(heads,),
        in_specs=[pl.BlockSpec((1, 1, period), lambda h: (h, 0, 0))],
        out_specs=pl.BlockSpec((1, 2 * ATT_BLOCK, ATT_BLOCK), lambda h: (h, 0, 0)),
        out_shape=jax.ShapeDtypeStruct((heads, 2 * ATT_BLOCK, ATT_BLOCK), F32),
        name="bias_strips",
    )(vec[:, None, :])


def _pad_cols(t, width):
    return jnp.pad(t, ((0, 0), (0, width - t.shape[1])))


def _pad_rows(t, height):
    return jnp.pad(t, ((0, height - t.shape[0]), (0, 0)))


def _hi_lo(w):
    hi = w.astype(BF16)
    return jnp.stack([hi, (w - hi.astype(F32)).astype(BF16)])


def _split_w_in(w):
    o = 0
    def take(n):
        nonlocal o
        s = w[:, o:o + n]
        o += n
        return s
    mq, mk, mv = take(512), take(512), take(512)
    r, wd, k, v, ad, gd = take(512), take(LORA_DECAY), take(512), take(512), take(LORA_AAA), take(LORA_GATE)
    dq, dc, diq, dik, diw = take(512), take(DSA_KV_RANK), take(256), take(IDX_DIM), take(IDX_HEADS)
    gate = take(3 * D_MODEL)
    w_tm = jnp.concatenate([gate, mk, r, k, v, _pad_cols(wd, 128), _pad_cols(ad, 128), _pad_cols(gd, 256),
                            dc, _pad_cols(dik, 128)], axis=1)
    w_cm = jnp.concatenate([mq, mv, dq, diq, diw], axis=1).T
    return w_tm.astype(BF16), w_cm.astype(BF16)


def _index_key_perm():
    p = np.zeros((2, 128, 128), np.float32)
    c = np.arange(IDX_DIM)
    p[0, c, c] = 1.0
    p[0, c, 2 * IDX_DIM + c] = 1.0
    p[1, c, IDX_DIM + c] = 1.0
    return jnp.asarray(p, BF16)


def _permute_mu(mu):
    r, wd, k, v, ad, gd = jnp.split(mu, np.cumsum([512, LORA_DECAY, 512, 512, LORA_AAA])[:].tolist())
    z = lambda n: jnp.zeros((n,), F32)
    return jnp.concatenate([r, k, v, wd, z(64), ad, z(64), gd, z(256 - LORA_GATE)])[None, :]


def kernel(x, w_in, norm_mix, norm_ffn, qk_norm, rel_bias, rwkv_mu, rwkv_w0, rwkv_w2, rwkv_a0, rwkv_a2, rwkv_g2, rwkv_kk, rwkv_ka, rwkv_rk, rwkv_ln_w, rwkv_ln_b, rwkv_v0, rwkv_va, rwkv_vb, dsa_kv_norm, dsa_kv_up, w_branch, w_o, w_ffn_in, w_ffn_out):
    B, S, D = x.shape
    depth = w_in.shape[0]
    T = B * S
    strips = _bias_strips(rel_bias * LOG2E)
    moba_bias, dsa_bias = strips[:N_HEADS], strips[N_HEADS:]
    head_of_lane = np.arange(BRANCH_W) // HEAD_DIM
    gmat = jnp.asarray(head_of_lane[:, None] == head_of_lane[None, :], BF16)
    tri = jnp.asarray(_chunk_cumsum_matrix(RWKV_TILE))
    perm = _index_key_perm()
    x2 = x.reshape(T, D)
    v_first = None
    for l in range(depth):
        w_tm, w_cm = _split_w_in(w_in[l])
        per_channel = lambda g: jnp.tile(g, N_HEADS)
        gq = jnp.stack([per_channel(qk_norm[l, 0]), per_channel(qk_norm[l, 2])]) * (LOG2E * HEAD_DIM ** -0.5)
        gq = jnp.broadcast_to(gq[:, :, None], (2, BRANCH_W, ATT_BLOCK))
        gk = jnp.stack([per_channel(qk_norm[l, 1]), per_channel(qk_norm[l, 3])])
        kv_up = dsa_kv_up[l].astype(BF16)
        (gate, mqt, mk, mvt, rw, dqt, dk, dvt, diqt, dik, diwt) = _proj_call(
            x2, norm_mix[l][None, :], w_tm, w_cm, dsa_kv_norm[l][None, :],
            kv_up[:, :BRANCH_W], kv_up[:, BRANCH_W:].T, gk, gq, gmat)
        o_moba = _moba_call(mqt, mk, mvt, moba_bias, B, S)
        o_dsa = _dsa_call(dqt, dk, dvt, diqt, dik, diwt, dsa_bias, perm, B, S)
        vec = jnp.stack([rwkv_w0[l], rwkv_a0[l], rwkv_kk[l], rwkv_ka[l]])
        vres = None
        if l > 0:
            vres = (v_first, rwkv_v0[l - 1][None, :], _pad_cols(rwkv_va[l - 1], 128).astype(BF16),
                    _pad_rows(rwkv_vb[l - 1], 128).astype(BF16))
        vec2 = jnp.stack([rwkv_ln_w[l], rwkv_ln_b[l], rwkv_rk[l].reshape(-1)])
        o_rwkv, v_new = _rwkv_call(
            rw, _permute_mu(rwkv_mu[l]), vec, _hi_lo(_pad_rows(rwkv_w2[l], 128)), _hi_lo(_pad_rows(rwkv_a2[l], 128)),
            _pad_rows(rwkv_g2[l], 256).astype(BF16), gmat, vec2, tri, B, S, vres)
        if l == 0:
            v_first = v_new
        x2 = _merge_ffn_call(x2, gate, o_moba, o_rwkv, o_dsa, w_branch[l].astype(BF16),
                             w_o[l].astype(BF16), norm_ffn[l][None, :],
                             w_ffn_in[l].astype(BF16), w_ffn_out[l].astype(BF16))
    return x2.reshape(B, S, D)
```
